```python
import jax, jax.numpy as jnp
from jax import lax
import numpy as np

D_MODEL = 4096
BATCH = 2
SEQ = 8192
DEPTH = 2

CONV_A_DIM = D_MODEL // 2
CONV_A_WIDTH = 3
CONV_B_DIM = D_MODEL // 2
CONV_B_WIDTH = 31
CONV_IN_DIM = 3 * CONV_A_DIM + 2 * CONV_B_DIM
MLA_HEADS = 32
QK_NOPE_DIM = 128
QK_ROPE_DIM = 64
QK_HEAD_DIM = QK_NOPE_DIM + QK_ROPE_DIM
V_HEAD_DIM = 128
Q_LORA_RANK = 1024
KV_LORA_RANK = 512
MLA_IN_DIM = Q_LORA_RANK + KV_LORA_RANK + QK_ROPE_DIM
ROPE_THETA = 10000.0
Q_BLOCK = 128
N_EXPERTS = 16
N_GROUPS = 4
EXPERTS_PER_GROUP = N_EXPERTS // N_GROUPS
TOP_K = 2
EXPERT_DIM = 1024
ROUTED_SCALE = 1.0
EPS = 1e-6
ADA_INIT = 0.3

kernel_name = 'hybrid_conv_mla_grouped_moe_adaln'

F32 = jnp.float32


def rms_norm(x, g):
    xf = x.astype(F32)
    y = xf * lax.rsqrt(jnp.mean(xf * xf, axis=-1, keepdims=True) + EPS)
    return (y * g.astype(F32)).astype(x.dtype)


def layer_norm(x, g, b):
    xf = x.astype(F32)
    mu = jnp.mean(xf, axis=-1, keepdims=True)
    var = jnp.mean(jnp.square(xf - mu), axis=-1, keepdims=True)
    y = (xf - mu) * lax.rsqrt(var + EPS)
    return (y * g.astype(F32) + b.astype(F32)).astype(x.dtype)


def ada_params(cond, w, b):
    m = cond @ w + b
    shift, scale, gate = jnp.split(m, 3, axis=-1)
    return shift[:, None, :], scale[:, None, :], gate[:, None, :]


def causal_depthwise_conv(x, w):
    k = w.shape[0]
    return lax.conv_general_dilated(
        x, w[:, None, :], window_strides=(1,), padding=((k - 1, 0),),
        dimension_numbers=('NWC', 'WIO', 'NWC'), feature_group_count=x.shape[-1])


def conv_mixers(h, w_in, conv_a, conv_b, conv_b_bias, ln_g, ln_b, w_out):
    proj = h @ w_in
    a_bg, a_cg, a_x, b_val, b_gate = jnp.split(
        proj, [CONV_A_DIM, 2 * CONV_A_DIM, 3 * CONV_A_DIM, 3 * CONV_A_DIM + CONV_B_DIM], axis=-1)
    y_a = a_bg * causal_depthwise_conv(a_cg * a_x, conv_a)
    u = b_val * jax.nn.sigmoid(b_gate)
    u = causal_depthwise_conv(u, conv_b) + conv_b_bias
    y_b = jax.nn.silu(layer_norm(u, ln_g, ln_b))
    return jnp.concatenate([y_a, y_b], axis=-1) @ w_out


def rope_tables(positions):
    inv_freq = 1.0 / (ROPE_THETA ** (jnp.arange(0, QK_ROPE_DIM, 2, dtype=F32) / QK_ROPE_DIM))
    ang = positions.astype(F32)[..., None] * inv_freq
    return jnp.cos(ang)[:, :, None, :], jnp.sin(ang)[:, :, None, :]


def apply_rope(x, cos, sin):
    xf = x.astype(F32)
    x1, x2 = jnp.split(xf, 2, axis=-1)
    return jnp.concatenate([x1 * cos - x2 * sin, x2 * cos + x1 * sin], axis=-1).astype(x.dtype)


def blocked_causal_attention(q, k, v):
    b, s, h, dq = q.shape
    nb = s // Q_BLOCK
    scale = dq ** -0.5
    qb = q.reshape(b, nb, Q_BLOCK, h, dq).swapaxes(0, 1)
    k_idx = jnp.arange(s)
    neg = jnp.finfo(F32).min

    def block(args):
        i, qi = args
        sc = jnp.einsum('bqhd,bkhd->bhqk', qi, k, preferred_element_type=F32) * scale
        q_idx = i * Q_BLOCK + jnp.arange(Q_BLOCK)
        mask = k_idx[None, :] <= q_idx[:, None]
        p = jax.nn.softmax(jnp.where(mask, sc, neg), axis=-1).astype(v.dtype)
        return jnp.einsum('bhqk,bkhd->bqhd', p, v)

    o = lax.map(block, (jnp.arange(nb), qb))
    return o.swapaxes(0, 1).reshape(b, s, h, v.shape[-1])


def mla(h, positions, w_in, q_a_norm, kv_a_norm, w_q_up, w_kv_up, q_norm, k_norm, w_out):
    b, s, _ = h.shape
    proj = h @ w_in
    c_q, c_kv, k_pe = jnp.split(proj, [Q_LORA_RANK, Q_LORA_RANK + KV_LORA_RANK], axis=-1)
    c_q = rms_norm(c_q, q_a_norm)
    c_kv = rms_norm(c_kv, kv_a_norm)
    q = (c_q @ w_q_up).reshape(b, s, MLA_HEADS, QK_HEAD_DIM)
    kv = (c_kv @ w_kv_up).reshape(b, s, MLA_HEADS, QK_NOPE_DIM + V_HEAD_DIM)
    k_nope, v = jnp.split(kv, [QK_NOPE_DIM], axis=-1)
    k_pe = jnp.broadcast_to(k_pe[:, :, None, :], (b, s, MLA_HEADS, QK_ROPE_DIM))
    k = jnp.concatenate([k_nope, k_pe], axis=-1)
    q = rms_norm(q, q_norm)
    k = rms_norm(k, k_norm)
    cos, sin = rope_tables(positions)
    q = jnp.concatenate([q[..., :QK_NOPE_DIM], apply_rope(q[..., QK_NOPE_DIM:], cos, sin)], axis=-1)
    k = jnp.concatenate([k[..., :QK_NOPE_DIM], apply_rope(k[..., QK_NOPE_DIM:], cos, sin)], axis=-1)
    o = blocked_causal_attention(q, k, v)
    return o.reshape(b, s, MLA_HEADS * V_HEAD_DIM) @ w_out


def grouped_moe(h, router_w, router_b, w_gate, w_up, w_down):
    b, s, d = h.shape
    t = h.reshape(b * s, d)
    affinity = jax.nn.sigmoid(jnp.dot(t, router_w, preferred_element_type=F32))
    sel = affinity + router_b.astype(F32)
    grouped = sel.reshape(-1, N_GROUPS, EXPERTS_PER_GROUP)
    group_score = lax.top_k(grouped, TOP_K)[0].sum(-1)
    best_group = jnp.argmax(group_score, axis=-1)
    in_group = (jnp.arange(N_EXPERTS) // EXPERTS_PER_GROUP)[None, :] == best_group[:, None]
    _, idx = lax.top_k(jnp.where(in_group, sel, -jnp.inf), TOP_K)
    w = jnp.take_along_axis(affinity, idx, axis=-1)
    w = w / jnp.sum(w, axis=-1, keepdims=True) * ROUTED_SCALE
    gates = jnp.sum(jax.nn.one_hot(idx, N_EXPERTS, dtype=F32) * w[..., None], axis=1).astype(t.dtype)
    out = jnp.zeros_like(t)
    for e in range(N_EXPERTS):
        he = jax.nn.silu(t @ w_gate[e]) * (t @ w_up[e])
        out = out + gates[:, e:e + 1] * (he @ w_down[e])
    return out.reshape(b, s, d)


def setup_inputs(seed: int = 0) -> dict:
    key = jax.random.key(seed)
    ks = iter(jax.random.split(key, 48))
    D = D_MODEL

    def nrm(shape, fan_in, mult=1.0):
        return jax.random.normal(next(ks), shape, F32) * (mult * fan_in ** -0.5)

    def gain(n):
        return 1.0 + 0.05 * jax.random.normal(next(ks), (n,), F32)

    def bias(n, s=0.02):
        return s * jax.random.normal(next(ks), (n,), F32)

    x = jax.random.normal(next(ks), (BATCH, SEQ, D), F32)
    c = jax.random.normal(next(ks), (BATCH, D), F32)
    offset = jax.random.randint(next(ks), (BATCH, 1), 0, 1024, dtype=jnp.int32)
    positions = offset + jnp.arange(SEQ, dtype=jnp.int32)[None, :]
    inp = {'x': x, 'c': c, 'positions': positions}
    inp['l0_mix_norm'] = gain(D)
    inp['l0_mix_ada_w'] = nrm((D, 3 * D), D, ADA_INIT)
    inp['l0_mix_ada_b'] = bias(3 * D)
    inp['l0_in_proj'] = nrm((D, CONV_IN_DIM), D)
    inp['l0_conv_a'] = nrm((CONV_A_WIDTH, CONV_A_DIM), CONV_A_WIDTH)
    inp['l0_conv_b'] = nrm((CONV_B_WIDTH, CONV_B_DIM), CONV_B_WIDTH)
    inp['l0_conv_b_bias'] = bias(CONV_B_DIM)
    inp['l0_conv_b_ln_g'] = gain(CONV_B_DIM)
    inp['l0_conv_b_ln_b'] = bias(CONV_B_DIM)
    inp['l0_out_proj'] = nrm((CONV_A_DIM + CONV_B_DIM, D), CONV_A_DIM + CONV_B_DIM)
    inp['l0_ffn_norm'] = gain(D)
    inp['l0_ffn_ada_w'] = nrm((D, 3 * D), D, ADA_INIT)
    inp['l0_ffn_ada_b'] = bias(3 * D)
    inp['l0_exp_gate'] = nrm((N_EXPERTS, D, EXPERT_DIM), D)
    inp['l0_exp_up'] = nrm((N_EXPERTS, D, EXPERT_DIM), D)
    inp['l0_exp_down'] = nrm((N_EXPERTS, EXPERT_DIM, D), EXPERT_DIM)
    inp['l1_mix_norm'] = gain(D)
    inp['l1_mix_ada_w'] = nrm((D, 3 * D), D, ADA_INIT)
    inp['l1_mix_ada_b'] = bias(3 * D)
    inp['l1_in_proj'] = nrm((D, MLA_IN_DIM), D)
    inp['l1_q_a_norm'] = gain(Q_LORA_RANK)
    inp['l1_kv_a_norm'] = gain(KV_LORA_RANK)
    inp['l1_q_up'] = nrm((Q_LORA_RANK, MLA_HEADS * QK_HEAD_DIM), Q_LORA_RANK)
    inp['l1_kv_up'] = nrm((KV_LORA_RANK, MLA_HEADS * (QK_NOPE_DIM + V_HEAD_DIM)), KV_LORA_RANK)
    inp['l1_q_norm'] = gain(QK_HEAD_DIM)
    inp['l1_k_norm'] = gain(QK_HEAD_DIM)
    inp['l1_out_proj'] = nrm((MLA_HEADS * V_HEAD_DIM, D), MLA_HEADS * V_HEAD_DIM)
    inp['l1_ffn_norm'] = gain(D)
    inp['l1_ffn_ada_w'] = nrm((D, 3 * D), D, ADA_INIT)
    inp['l1_ffn_ada_b'] = bias(3 * D)
    inp['l1_exp_gate'] = nrm((N_EXPERTS, D, EXPERT_DIM), D)
    inp['l1_exp_up'] = nrm((N_EXPERTS, D, EXPERT_DIM), D)
    inp['l1_exp_down'] = nrm((N_EXPERTS, EXPERT_DIM, D), EXPERT_DIM)
    inp['router_w'] = nrm((D, N_EXPERTS), D)
    inp['router_b'] = bias(N_EXPERTS, 0.01)
    return inp


def reference(x, c, positions,
              l0_mix_norm, l0_mix_ada_w, l0_mix_ada_b, l0_in_proj, l0_conv_a, l0_conv_b,
              l0_conv_b_bias, l0_conv_b_ln_g, l0_conv_b_ln_b, l0_out_proj,
              l0_ffn_norm, l0_ffn_ada_w, l0_ffn_ada_b, l0_exp_gate, l0_exp_up, l0_exp_down,
              l1_mix_norm, l1_mix_ada_w, l1_mix_ada_b, l1_in_proj, l1_q_a_norm, l1_kv_a_norm,
              l1_q_up, l1_kv_up, l1_q_norm, l1_k_norm, l1_out_proj,
              l1_ffn_norm, l1_ffn_ada_w, l1_ffn_ada_b, l1_exp_gate, l1_exp_up, l1_exp_down,
              router_w, router_b):
    mix_mod = [(l0_mix_norm, l0_mix_ada_w, l0_mix_ada_b), (l1_mix_norm, l1_mix_ada_w, l1_mix_ada_b)]
    ffn_mod = [(l0_ffn_norm, l0_ffn_ada_w, l0_ffn_ada_b), (l1_ffn_norm, l1_ffn_ada_w, l1_ffn_ada_b)]
    mixers = [(l0_in_proj, l0_conv_a, l0_conv_b, l0_conv_b_bias, l0_conv_b_ln_g, l0_conv_b_ln_b, l0_out_proj),
              (l1_in_proj, l1_q_a_norm, l1_kv_a_norm, l1_q_up, l1_kv_up, l1_q_norm, l1_k_norm, l1_out_proj)]
    experts = [(l0_exp_gate, l0_exp_up, l0_exp_down), (l1_exp_gate, l1_exp_up, l1_exp_down)]
    cond = jax.nn.silu(c)
    h = x
    for layer in range(DEPTH):
        g, aw, ab = mix_mod[layer]
        shift, scale, gate = ada_params(cond, aw, ab)
        u = rms_norm(h, g) * (1.0 + scale) + shift
        if layer % 2 == 0:
            y = conv_mixers(u, *mixers[layer])
        else:
            y = mla(u, positions, *mixers[layer])
        h = h + gate * y
        g, aw, ab = ffn_mod[layer]
        shift, scale, gate = ada_params(cond, aw, ab)
        u = rms_norm(h, g) * (1.0 + scale) + shift
        h = h + gate * grouped_moe(u, router_w, router_b, *experts[layer])
    return h
```

```python
import functools

import jax
import jax.numpy as jnp
from jax import lax
from jax.experimental import pallas as pl
from jax.experimental.pallas import tpu as pltpu

F32 = jnp.float32
BF16 = jnp.bfloat16
I32 = jnp.int32

EPS = 1e-6
ROPE_THETA = 10000.0

N_EXPERTS = 16
N_GROUPS = 4
GROUP_SIZE = N_EXPERTS // N_GROUPS
CONV_A_WIDTH = 3
CONV_B_WIDTH = 31
CONV_HALO = 32
MLA_HEADS = 32
QK_NOPE = 128
QK_ROPE = 64
QK_HEAD = QK_NOPE + QK_ROPE
V_HEAD = 128
HEAD_SLAB = 256
LANES = 128

MOE_TILE = 512
VMEM_LIMIT = 56 * 1024 * 1024


def _cparams(n_axes, vmem=VMEM_LIMIT):
    return pltpu.CompilerParams(dimension_semantics=("arbitrary",) * n_axes, vmem_limit_bytes=vmem)


def _dot(a, b):
    return jnp.dot(a, b, preferred_element_type=F32)


def _dot_nt(a, b):
    return lax.dot_general(a, b, (((1,), (1,)), ((), ())), preferred_element_type=F32)


def _sigmoid(x):
    return 1.0 / (1.0 + jnp.exp(-x))


def _ada_kernel(c_ref, w_ref, b_ref, o_ref):
    c = c_ref[...]
    cond = (c * _sigmoid(c)).astype(BF16)
    o_ref[...] = _dot(cond, w_ref[...].astype(BF16)) + b_ref[...]


def _ada(c8, w, b):
    d, n = w.shape
    tn = 512
    return pl.pallas_call(
        _ada_kernel,
        grid=(n // tn,),
        in_specs=[pl.BlockSpec((8, d), lambda j: (0, 0)),
                  pl.BlockSpec((d, tn), lambda j: (0, j)),
                  pl.BlockSpec((1, tn), lambda j: (0, j))],
        out_specs=pl.BlockSpec((8, tn), lambda j: (0, j)),
        out_shape=jax.ShapeDtypeStruct((8, n), F32),
        compiler_params=_cparams(1),
        name="ada",
    )(c8, w, b.reshape(1, n))


def _split_ada(m, batch, d):
    m = m[:batch]
    return (m[:, None, 0:d], m[:, None, d:2 * d], m[:, None, 2 * d:3 * d])


def _modulated_norm(h, g, scale, shift):
    inv = lax.rsqrt(jnp.mean(h * h, axis=-1, keepdims=True) + EPS)
    return (h * inv) * (g * (1.0 + scale)) + shift


def _norm_mod_kernel(h_ref, g_ref, sc_ref, sh_ref, o_ref):
    o_ref[...] = _modulated_norm(h_ref[...], g_ref[...], sc_ref[...], sh_ref[...]).astype(o_ref.dtype)


def _norm_mod(h, g, scale, shift, out_dtype, tm=256):
    b, s, d = h.shape
    return pl.pallas_call(
        _norm_mod_kernel,
        grid=(b, s // tm),
        in_specs=[pl.BlockSpec((None, tm, d), lambda bi, i: (bi, i, 0)),
                  pl.BlockSpec((1, d), lambda bi, i: (0, 0)),
                  pl.BlockSpec((None, 1, d), lambda bi, i: (bi, 0, 0)),
                  pl.BlockSpec((None, 1, d), lambda bi, i: (bi, 0, 0))],
        out_specs=pl.BlockSpec((None, tm, d), lambda bi, i: (bi, i, 0)),
        out_shape=jax.ShapeDtypeStruct((b, s, d), out_dtype),
        compiler_params=_cparams(2),
        name="norm_mod",
    )(h, g.reshape(1, d), scale, shift)


def _l0_in_kernel(u_ref, w_bg, w_cg, w_x, w_val, w_gate, o_bg, o_v, o_ug):
    u = u_ref[...]
    o_bg[...] = _dot(u, w_bg[...]).astype(BF16)
    o_v[...] = (_dot(u, w_cg[...]) * _dot(u, w_x[...])).astype(BF16)
    o_ug[...] = (_dot(u, w_val[...]) * _sigmoid(_dot(u, w_gate[...]))).astype(BF16)


def _l0_in(u, w_in, c, tm=512, tn=256):
    t, d = u.shape
    nj = c // tn
    w_specs = [pl.BlockSpec((d, tn), functools.partial(lambda i, j, g: (0, g * nj + j), g=g)) for g in range(5)]
    out_spec = pl.BlockSpec((tm, tn), lambda i, j: (i, j))
    out_sds = jax.ShapeDtypeStruct((t, c), BF16)
    return pl.pallas_call(
        _l0_in_kernel,
        grid=(t // tm, nj),
        in_specs=[pl.BlockSpec((tm, d), lambda i, j: (i, 0))] + w_specs,
        out_specs=[out_spec] * 3,
        out_shape=[out_sds] * 3,
        compiler_params=_cparams(2),
        name="l0_in",
    )(u, w_in, w_in, w_in, w_in, w_in)


def _l0_conv_kernel(bg_ref, v_ref, ug_ref, vh_ref, ugh_ref, wa_ref, wb_ref, bb_ref, lg_ref, lb_ref,
                    o_ref, vs_ref, us_ref, ub_ref, *, ts, c, rc):
    first = pl.program_id(1) == 0
    keep = jnp.where(first, 0.0, 1.0)
    vs_ref[0:CONV_HALO, :] = vh_ref[...].astype(F32) * keep
    us_ref[0:CONV_HALO, :] = ugh_ref[...].astype(F32) * keep
    vs_ref[CONV_HALO:, :] = v_ref[...].astype(F32)
    us_ref[CONV_HALO:, :] = ug_ref[...].astype(F32)

    win = rc + CONV_HALO

    def causal_conv(w_ref, width, window, cols, acc):
        for res in range(8):
            taps = [k for k in range(width) if (CONV_HALO - (width - 1) + k) % 8 == res]
            if not taps:
                continue
            shifted = window if res == 0 else pltpu.roll(window, win - res, 0)
            for k in taps:
                a0 = (CONV_HALO - (width - 1) + k) - res
                acc = acc + w_ref[k:k + 1, cols] * shifted[a0:a0 + rc, :]
        return acc

    def rows(r, carry):
        r0 = pl.multiple_of(r * rc, rc)
        for cj in range(c // LANES):
            cols = pl.ds(cj * LANES, LANES)
            acc = causal_conv(wa_ref, CONV_A_WIDTH, vs_ref[pl.ds(r0, win), cols], cols, jnp.zeros((rc, LANES), F32))
            o_ref[pl.ds(r0, rc), cols] = (bg_ref[pl.ds(r0, rc), cols].astype(F32) * acc).astype(BF16)
            acc = causal_conv(wb_ref, CONV_B_WIDTH, us_ref[pl.ds(r0, win), cols], cols,
                              jnp.zeros((rc, LANES), F32) + bb_ref[:, cols])
            ub_ref[pl.ds(r0, rc), cols] = acc
        return carry

    lax.fori_loop(0, ts // rc, rows, 0)

    ub = ub_ref[...]
    mu = jnp.mean(ub, axis=-1, keepdims=True)
    xc = ub - mu
    var = jnp.mean(xc * xc, axis=-1, keepdims=True)
    y = xc * lax.rsqrt(var + EPS) * lg_ref[...] + lb_ref[...]
    o_ref[:, c:] = (y * _sigmoid(y)).astype(BF16)


def _l0_conv(a_bg, v, ug, conv_a, conv_b, conv_b_bias, ln_g, ln_b, ts=256, rc=32):
    b, s, c = v.shape
    hb = ts // CONV_HALO
    main = pl.BlockSpec((None, ts, c), lambda bi, i: (bi, i, 0))
    halo = pl.BlockSpec((None, CONV_HALO, c), lambda bi, i: (bi, jnp.maximum(i * hb - 1, 0), 0))
    wb_rows = 32
    conv_b_pad = jnp.zeros((wb_rows, c), F32).at[:CONV_B_WIDTH].set(conv_b)
    conv_a_pad = jnp.zeros((8, c), F32).at[:CONV_A_WIDTH].set(conv_a)
    full = lambda r: pl.BlockSpec((r, c), lambda bi, i: (0, 0))
    return pl.pallas_call(
        functools.partial(_l0_conv_kernel, ts=ts, c=c, rc=rc),
        grid=(b, s // ts),
        in_specs=[main, main, main, halo, halo, full(8), full(wb_rows), full(1), full(1), full(1)],
        out_specs=pl.BlockSpec((None, ts, 2 * c), lambda bi, i: (bi, i, 0)),
        out_shape=jax.ShapeDtypeStruct((b, s, 2 * c), BF16),
        scratch_shapes=[pltpu.VMEM((CONV_HALO + ts, c), F32),
                        pltpu.VMEM((CONV_HALO + ts, c), F32),
                        pltpu.VMEM((ts, c), F32)],
        compiler_params=_cparams(2),
        name="l0_conv",
    )(a_bg, v, ug, v, ug, conv_a_pad, conv_b_pad, conv_b_bias.reshape(1, c), ln_g.reshape(1, c), ln_b.reshape(1, c))


def _out_res_kernel(a_ref, w_ref, h_ref, gate_ref, o_ref):
    o_ref[...] = h_ref[...] + gate_ref[...] * _dot(a_ref[...], w_ref[...])


def _out_res(a, w, h, gate, tm=1024, tn=512):
    b, s, k = a.shape
    d = w.shape[1]
    return pl.pallas_call(
        _out_res_kernel,
        grid=(b, s // tm, d // tn),
        in_specs=[pl.BlockSpec((None, tm, k), lambda bi, i, j: (bi, i, 0)),
                  pl.BlockSpec((k, tn), lambda bi, i, j: (0, j)),
                  pl.BlockSpec((None, tm, tn), lambda bi, i, j: (bi, i, j)),
                  pl.BlockSpec((None, 1, tn), lambda bi, i, j: (bi, 0, j))],
        out_specs=pl.BlockSpec((None, tm, tn), lambda bi, i, j: (bi, i, j)),
        out_shape=jax.ShapeDtypeStruct((b, s, d), F32),
        compiler_params=_cparams(3),
        name="out_res",
    )(a, w, h, gate)


def _route_kernel(h_ref, g_ref, sc_ref, sh_ref, rw_ref, rb_ref, u_ref, ri_ref, rwt_ref, cnt_ref,
                  tri_ref, carry_ref, *, tm):
    step0 = jnp.logical_and(pl.program_id(0) == 0, pl.program_id(1) == 0)

    @pl.when(step0)
    def _():
        carry_ref[...] = jnp.zeros_like(carry_ref)
        src = lax.broadcasted_iota(I32, (tm, tm), 0)
        dst = lax.broadcasted_iota(I32, (tm, tm), 1)
        tri_ref[...] = jnp.where(src < dst, 1.0, 0.0).astype(BF16)

    u = _modulated_norm(h_ref[...], g_ref[...], sc_ref[...], sh_ref[...])
    u_ref[...] = u
    u_hi = u.astype(BF16)
    u_lo = (u - u_hi.astype(F32)).astype(BF16)
    rw = rw_ref[...]
    rw_hi = rw.astype(BF16)
    rw_lo = (rw - rw_hi.astype(F32)).astype(BF16)
    logits = _dot_nt(rw_hi, u_hi) + (_dot_nt(rw_hi, u_lo) + _dot_nt(rw_lo, u_hi))
    aff = _sigmoid(logits)
    sel = aff + rb_ref[...]
    row = lax.broadcasted_iota(I32, (N_EXPERTS, tm), 0)

    best, best_g = None, None
    for g in range(N_GROUPS):
        s = [sel[g * GROUP_SIZE + j:g * GROUP_SIZE + j + 1, :] for j in range(GROUP_SIZE)]
        score = None
        for a in range(GROUP_SIZE):
            for bb in range(a + 1, GROUP_SIZE):
                pair = s[a] + s[bb]
                score = pair if score is None else jnp.maximum(score, pair)
        if best is None:
            best, best_g = score, jnp.zeros((1, tm), I32)
        else:
            upd = score > best
            best = jnp.where(upd, score, best)
            best_g = jnp.where(upd, g, best_g)

    neg = -jnp.inf
    masked = jnp.where((row // GROUP_SIZE) == best_g, sel, neg)
    m1 = jnp.max(masked, axis=0, keepdims=True)
    i1 = jnp.min(jnp.where(masked == m1, row, N_EXPERTS), axis=0, keepdims=True)
    masked2 = jnp.where(row == i1, neg, masked)
    m2 = jnp.max(masked2, axis=0, keepdims=True)
    i2 = jnp.min(jnp.where(masked2 == m2, row, N_EXPERTS), axis=0, keepdims=True)
    hit1 = row == i1
    hit2 = row == i2
    a1 = jnp.sum(jnp.where(hit1, aff, 0.0), axis=0, keepdims=True)
    a2 = jnp.sum(jnp.where(hit2, aff, 0.0), axis=0, keepdims=True)
    den = a1 + a2
    w1 = a1 / den
    w2 = a2 / den

    onehot = jnp.where(jnp.logical_or(hit1, hit2), 1.0, 0.0)
    before = _dot(onehot.astype(BF16), tri_ref[...]) + carry_ref[...]
    r1 = jnp.sum(jnp.where(hit1, before, 0.0), axis=0, keepdims=True).astype(I32)
    r2 = jnp.sum(jnp.where(hit2, before, 0.0), axis=0, keepdims=True).astype(I32)
    carry_ref[...] = carry_ref[...] + jnp.sum(onehot, axis=1, keepdims=True)

    zi = jnp.zeros((4, tm), I32)
    ri_ref[...] = jnp.concatenate([i1, i2, r1, r2, zi], axis=0)
    rwt_ref[...] = jnp.concatenate([w1, w2, jnp.zeros((6, tm), F32)], axis=0)
    cnt_ref[...] = jnp.broadcast_to(carry_ref[...], cnt_ref.shape)


def _route(h, g, scale, shift, router_w, router_b, tm=512):
    b, s, d = h.shape
    t = b * s
    n_i = s // tm
    return pl.pallas_call(
        functools.partial(_route_kernel, tm=tm),
        grid=(b, n_i),
        in_specs=[pl.BlockSpec((None, tm, d), lambda bi, i: (bi, i, 0)),
                  pl.BlockSpec((1, d), lambda bi, i: (0, 0)),
                  pl.BlockSpec((None, 1, d), lambda bi, i: (bi, 0, 0)),
                  pl.BlockSpec((None, 1, d), lambda bi, i: (bi, 0, 0)),
                  pl.BlockSpec((N_EXPERTS, d), lambda bi, i: (0, 0)),
                  pl.BlockSpec((N_EXPERTS, 1), lambda bi, i: (0, 0))],
        out_specs=[pl.BlockSpec((None, tm, d), lambda bi, i: (bi, i, 0)),
                   pl.BlockSpec((8, tm), lambda bi, i: (0, bi * n_i + i)),
                   pl.BlockSpec((8, tm), lambda bi, i: (0, bi * n_i + i)),
                   pl.BlockSpec((N_EXPERTS, LANES), lambda bi, i: (0, 0))],
        out_shape=[jax.ShapeDtypeStruct((b, s, d), F32),
                   jax.ShapeDtypeStruct((8, t), I32),
                   jax.ShapeDtypeStruct((8, t), F32),
                   jax.ShapeDtypeStruct((N_EXPERTS, LANES), F32)],
        scratch_shapes=[pltpu.VMEM((tm, tm), BF16), pltpu.VMEM((N_EXPERTS, 1), F32)],
        compiler_params=_cparams(2),
        name="route",
    )(h, g.reshape(1, d), scale, shift, router_w.T, router_b.reshape(N_EXPERTS, 1))


def _dispatch_kernel(off_ref, cnt_ref, nt_ref, ri_ref, u_hbm, xs_hbm, zero_ref, sem, *, tb):
    i = pl.program_id(0)
    zrows = zero_ref.shape[0]

    def row_copy(tok, pos, k):
        return pltpu.make_async_copy(u_hbm.at[pl.ds(tok, 1)], xs_hbm.at[pl.ds(pos, 1)], sem.at[k])

    def pad_copy(pos):
        return pltpu.make_async_copy(zero_ref.at[pl.ds(0, 1)], xs_hbm.at[pl.ds(pos, 1)], sem.at[2])

    def tail_copy(blk):
        start = pl.multiple_of(nt_ref[0] * MOE_TILE + blk * zrows, zrows)
        return pltpu.make_async_copy(zero_ref, xs_hbm.at[pl.ds(start, zrows)], sem.at[2])

    @pl.when(i == 0)
    def _():
        zero_ref[...] = jnp.zeros_like(zero_ref)
        n_tail = (xs_hbm.shape[0] // MOE_TILE - nt_ref[0]) * (MOE_TILE // zrows)
        lax.fori_loop(0, n_tail, lambda blk, c: (tail_copy(blk).start(), c)[1], 0)
        lax.fori_loop(0, n_tail, lambda blk, c: (tail_copy(blk).wait(), c)[1], 0)
        for e in range(N_EXPERTS):
            n_pad = (MOE_TILE - (cnt_ref[e] & (MOE_TILE - 1))) & (MOE_TILE - 1)
            base = off_ref[e] + cnt_ref[e]

            def start(r, c, base=base):
                pad_copy(base + r).start()
                return c

            def wait(r, c, base=base):
                pad_copy(base + r).wait()
                return c

            lax.fori_loop(0, n_pad, start, 0)
            lax.fori_loop(0, n_pad, wait, 0)

    def start(tt, c):
        for k in range(2):
            pos = off_ref[ri_ref[k, tt]] + ri_ref[2 + k, tt]
            row_copy(i * tb + tt, pos, k).start()
        return c

    def wait(tt, c):
        for k in range(2):
            row_copy(0, 0, k).wait()
        return c

    lax.fori_loop(0, tb, start, 0)
    lax.fori_loop(0, tb, wait, 0)


def _dispatch(u2d, ri, off, counts, n_tiles, n_rows, tb=1024):
    t, d = u2d.shape
    grid_spec = pltpu.PrefetchScalarGridSpec(
        num_scalar_prefetch=3,
        grid=(t // tb,),
        in_specs=[pl.BlockSpec((8, tb), lambda i, off, cnt, nt: (0, i), memory_space=pltpu.SMEM),
                  pl.BlockSpec(memory_space=pl.ANY)],
        out_specs=pl.BlockSpec(memory_space=pl.ANY),
        scratch_shapes=[pltpu.VMEM((64, d), u2d.dtype), pltpu.SemaphoreType.DMA((3,))],
    )
    return pl.pallas_call(
        functools.partial(_dispatch_kernel, tb=tb),
        grid_spec=grid_spec,
        out_shape=jax.ShapeDtypeStruct((n_rows, d), u2d.dtype),
        compiler_params=_cparams(1),
        name="dispatch",
    )(off, counts, n_tiles, ri, u2d)


def _experts_kernel(te_ref, nt_ref, xs_ref, wg_ref, wu_ref, wd_ref, ys_ref, xb_ref):
    j = pl.program_id(0)
    cidx = pl.program_id(1)

    @pl.when(j < nt_ref[0])
    def _():
        @pl.when(cidx == 0)
        def _():
            xb_ref[...] = xs_ref[...].astype(BF16)

        x = xb_ref[...]
        gate = _dot(x, wg_ref[...])
        up = _dot(x, wu_ref[...])
        he = (gate * _sigmoid(gate) * up).astype(BF16)
        d = ys_ref.shape[1]
        dn = min(d, 1024)
        for dj in range(d // dn):
            cols = pl.ds(dj * dn, dn)
            y = _dot(he, wd_ref[:, cols])

            @pl.when(cidx == 0)
            def _():
                ys_ref[:, cols] = y

            @pl.when(cidx != 0)
            def _():
                ys_ref[:, cols] = ys_ref[:, cols] + y

    @pl.when(jnp.logical_and(j >= nt_ref[0], cidx == 0))
    def _():
        ys_ref[...] = jnp.zeros_like(ys_ref)


def _experts(xs, w_gate, w_up, w_down, tile_expert, n_tiles, tf=256):
    n_rows, d = xs.shape
    f = w_gate.shape[2]
    nt_max = n_rows // MOE_TILE
    nc = f // tf

    def tile(j, c, te, nt):
        return jnp.minimum(j, nt[0] - 1)

    def chunk(j, c, te, nt):
        return jnp.where(j < nt[0], c, nc - 1)

    grid_spec = pltpu.PrefetchScalarGridSpec(
        num_scalar_prefetch=2,
        grid=(nt_max, nc),
        in_specs=[pl.BlockSpec((MOE_TILE, d), lambda j, c, te, nt: (tile(j, c, te, nt), 0)),
                  pl.BlockSpec((None, d, tf), lambda j, c, te, nt: (te[tile(j, c, te, nt)], 0, chunk(j, c, te, nt))),
                  pl.BlockSpec((None, d, tf), lambda j, c, te, nt: (te[tile(j, c, te, nt)], 0, chunk(j, c, te, nt))),
                  pl.BlockSpec((None, tf, d), lambda j, c, te, nt: (te[tile(j, c, te, nt)], chunk(j, c, te, nt), 0))],
        out_specs=pl.BlockSpec((MOE_TILE, d), lambda j, c, te, nt: (j, 0)),
        scratch_shapes=[pltpu.VMEM((MOE_TILE, d), BF16)],
    )
    return pl.pallas_call(
        _experts_kernel,
        grid_spec=grid_spec,
        out_shape=jax.ShapeDtypeStruct((n_rows, d), F32),
        compiler_params=_cparams(2),
        name="experts",
    )(tile_expert, n_tiles, xs, w_gate, w_up, w_down)


def _combine_kernel(off_ref, ri_ref, w_ref, h_ref, gate_ref, ys_hbm, o_ref, y0_ref, y1_ref, sem, *, tc):
    bufs = (y0_ref, y1_ref)

    def row_copy(pos, tt, k):
        return pltpu.make_async_copy(ys_hbm.at[pl.ds(pos, 1)], bufs[k].at[pl.ds(tt, 1)], sem.at[k])

    def start(tt, c):
        for k in range(2):
            pos = off_ref[ri_ref[k, tt]] + ri_ref[2 + k, tt]
            row_copy(pos, tt, k).start()
        return c

    def wait(tt, c):
        for k in range(2):
            row_copy(0, 0, k).wait()
        return c

    lax.fori_loop(0, tc, start, 0)
    lax.fori_loop(0, tc, wait, 0)
    w = w_ref[...]
    y = w[:, 0:1] * y0_ref[...] + w[:, 1:2] * y1_ref[...]
    o_ref[...] = h_ref[...] + gate_ref[...] * y


def _combine(h, gate, ys, ri, wts, off, tc=256):
    b, s, d = h.shape
    n_i = s // tc
    grid_spec = pltpu.PrefetchScalarGridSpec(
        num_scalar_prefetch=1,
        grid=(b, n_i),
        in_specs=[pl.BlockSpec((8, tc), lambda bi, i, off: (0, bi * n_i + i), memory_space=pltpu.SMEM),
                  pl.BlockSpec((tc, 2), lambda bi, i, off: (bi * n_i + i, 0)),
                  pl.BlockSpec((None, tc, d), lambda bi, i, off: (bi, i, 0)),
                  pl.BlockSpec((None, 1, d), lambda bi, i, off: (bi, 0, 0)),
                  pl.BlockSpec(memory_space=pl.ANY)],
        out_specs=pl.BlockSpec((None, tc, d), lambda bi, i, off: (bi, i, 0)),
        scratch_shapes=[pltpu.VMEM((tc, d), F32), pltpu.VMEM((tc, d), F32), pltpu.SemaphoreType.DMA((2,))],
    )
    return pl.pallas_call(
        functools.partial(_combine_kernel, tc=tc),
        grid_spec=grid_spec,
        out_shape=jax.ShapeDtypeStruct((b, s, d), F32),
        compiler_params=_cparams(2),
        name="combine",
    )(off, ri, wts, h, gate, ys)


def _moe(h, g, scale, shift, gate, router_w, router_b, w_gate, w_up, w_down):
    b, s, d = h.shape
    t = b * s
    nt_max = (2 * t) // MOE_TILE + N_EXPERTS
    u, ri, rwt, cnt = _route(h, g, scale, shift, router_w, router_b)
    counts = cnt[:, 0].astype(I32)
    tiles_per = (counts + MOE_TILE - 1) // MOE_TILE
    tile_end = jnp.cumsum(tiles_per)
    off = (tile_end - tiles_per) * MOE_TILE
    tile_expert = jnp.minimum(
        jnp.searchsorted(tile_end, jnp.arange(nt_max, dtype=I32), side="right"), N_EXPERTS - 1).astype(I32)
    n_tiles = tile_end[-1:].astype(I32)
    xs = _dispatch(u.reshape(t, d), ri, off, counts, n_tiles, nt_max * MOE_TILE)
    ys = _experts(xs, w_gate, w_up, w_down, tile_expert, n_tiles)
    return _combine(h, gate, ys, ri, rwt[0:2].T, off)


def _rope_tables_kernel(pos_ref, invf_ref, c_ref, s1_ref, s2_ref):
    ang = pos_ref[...].astype(F32) * invf_ref[...]
    lane = lax.broadcasted_iota(I32, ang.shape, 1)
    cos = jnp.cos(ang)
    sin = jnp.sin(ang)
    half = QK_ROPE // 2
    c_ref[...] = jnp.where(lane < QK_ROPE, cos, 0.0)
    s1_ref[...] = jnp.where(lane < half, -sin, 0.0)
    s2_ref[...] = jnp.where(jnp.logical_and(lane >= half, lane < QK_ROPE), sin, 0.0)


def _rope_tables(positions, tm=1024):
    t = positions.size
    half = QK_ROPE // 2
    inv_freq = 1.0 / (ROPE_THETA ** (jnp.arange(0, QK_ROPE, 2, dtype=F32) / QK_ROPE))
    invf = jnp.concatenate([inv_freq, inv_freq, jnp.zeros((LANES - 2 * half,), F32)]).reshape(1, LANES)
    spec = pl.BlockSpec((tm, LANES), lambda i: (i, 0))
    sds = jax.ShapeDtypeStruct((t, LANES), F32)
    return pl.pallas_call(
        _rope_tables_kernel,
        grid=(t // tm,),
        in_specs=[pl.BlockSpec((tm, 1), lambda i: (i, 0)), pl.BlockSpec((1, LANES), lambda i: (0, 0))],
        out_specs=[spec] * 3,
        out_shape=[sds] * 3,
        compiler_params=_cparams(1),
        name="rope_tables",
    )(positions.reshape(t, 1), invf)


def _rope(p, c, s1, s2):
    return p * c + pltpu.roll(p, 96, 1) * s1 + pltpu.roll(p, 32, 1) * s2


def _mla_in_kernel(u_ref, w_ref, gq_ref, gkv_ref, gkpe_ref, c_ref, s1_ref, s2_ref,
                   cq_ref, ckv_ref, kr_ref, kss_ref, *, q_rank, kv_rank):
    proj = _dot(u_ref[...], w_ref[...])
    cq = proj[:, :q_rank]
    cq_ref[...] = (cq * lax.rsqrt(jnp.mean(cq * cq, axis=-1, keepdims=True) + EPS) * gq_ref[...]).astype(BF16)
    ckv = proj[:, q_rank:q_rank + kv_rank]
    ckv_ref[...] = (ckv * lax.rsqrt(jnp.mean(ckv * ckv, axis=-1, keepdims=True) + EPS) * gkv_ref[...]).astype(BF16)
    kpe = proj[:, q_rank + kv_rank:]
    kss_ref[...] = jnp.broadcast_to(jnp.sum(kpe * kpe, axis=-1, keepdims=True), kss_ref.shape)
    kr_ref[...] = _rope(kpe * gkpe_ref[...], c_ref[...], s1_ref[...], s2_ref[...])


def _mla_in(u, w_in_pad, gq, gkv, gkpe, tabs, q_rank, kv_rank, tm=512):
    t, d = u.shape
    n = w_in_pad.shape[1]
    row = lambda w: pl.BlockSpec((tm, w), lambda i: (i, 0))
    const = lambda w: pl.BlockSpec((1, w), lambda i: (0, 0))
    return pl.pallas_call(
        functools.partial(_mla_in_kernel, q_rank=q_rank, kv_rank=kv_rank),
        grid=(t // tm,),
        in_specs=[row(d), pl.BlockSpec((d, n), lambda i: (0, 0)), const(q_rank), const(kv_rank), const(LANES),
                  row(LANES), row(LANES), row(LANES)],
        out_specs=[row(q_rank), row(kv_rank), row(LANES), row(LANES)],
        out_shape=[jax.ShapeDtypeStruct((t, q_rank), BF16), jax.ShapeDtypeStruct((t, kv_rank), BF16),
                   jax.ShapeDtypeStruct((t, LANES), F32), jax.ShapeDtypeStruct((t, LANES), F32)],
        compiler_params=_cparams(1),
        name="mla_in",
    )(u, w_in_pad, gq.reshape(1, -1), gkv.reshape(1, -1), gkpe, *tabs)


def _q_up_kernel(cq_ref, w_ref, g_ref, c_ref, s1_ref, s2_ref, q_ref, *, hb):
    cq = cq_ref[...]
    g = g_ref[...]
    for hh in range(hb):
        slab = _dot(cq, w_ref[:, hh * HEAD_SLAB:(hh + 1) * HEAD_SLAB])
        inv = lax.rsqrt(jnp.sum(slab * slab, axis=-1, keepdims=True) * (1.0 / QK_HEAD) + EPS)
        y = slab * inv * g
        rot = _rope(y[:, QK_NOPE:], c_ref[...], s1_ref[...], s2_ref[...])
        q_ref[hh] = jnp.concatenate([y[:, :QK_NOPE], rot], axis=1).astype(BF16)


def _q_up(cq, w_q_slab, g_slab, tabs, b, s, tm=1024, hb=8):
    t, r = cq.shape
    n_i = s // tm
    row = lambda w: pl.BlockSpec((tm, w), lambda bi, i, hj: (bi * n_i + i, 0))
    return pl.pallas_call(
        functools.partial(_q_up_kernel, hb=hb),
        grid=(b, n_i, MLA_HEADS // hb),
        in_specs=[row(r), pl.BlockSpec((r, hb * HEAD_SLAB), lambda bi, i, hj: (0, hj)),
                  pl.BlockSpec((1, HEAD_SLAB), lambda bi, i, hj: (0, 0)), row(LANES), row(LANES), row(LANES)],
        out_specs=pl.BlockSpec((None, hb, tm, HEAD_SLAB), lambda bi, i, hj: (bi, hj, i, 0)),
        out_shape=jax.ShapeDtypeStruct((b, MLA_HEADS, s, HEAD_SLAB), BF16),
        compiler_params=_cparams(3),
        name="q_up",
    )(cq, w_q_slab, g_slab, *tabs)


def _kv_up_kernel(ckv_ref, w_ref, g_ref, kr_ref, kss_ref, k_ref, v_ref, *, hb):
    ckv = ckv_ref[...]
    g = g_ref[...]
    kr = kr_ref[...]
    kss = kss_ref[:, 0:1]
    for hh in range(hb):
        slab = _dot(ckv, w_ref[:, hh * HEAD_SLAB:(hh + 1) * HEAD_SLAB])
        kn = slab[:, :QK_NOPE]
        inv = lax.rsqrt((jnp.sum(kn * kn, axis=-1, keepdims=True) + kss) * (1.0 / QK_HEAD) + EPS)
        k_ref[hh] = jnp.concatenate([kn * inv * g, kr * inv], axis=1).astype(BF16)
        v_ref[hh] = slab[:, QK_NOPE:].astype(BF16)


def _kv_up(ckv, w_kv, g_nope, kr, kss, b, s, tm=1024, hb=8):
    t, r = ckv.shape
    n_i = s // tm
    row = lambda w: pl.BlockSpec((tm, w), lambda bi, i, hj: (bi * n_i + i, 0))
    return pl.pallas_call(
        functools.partial(_kv_up_kernel, hb=hb),
        grid=(b, n_i, MLA_HEADS // hb),
        in_specs=[row(r), pl.BlockSpec((r, hb * HEAD_SLAB), lambda bi, i, hj: (0, hj)),
                  pl.BlockSpec((1, QK_NOPE), lambda bi, i, hj: (0, 0)), row(LANES), row(LANES)],
        out_specs=[pl.BlockSpec((None, hb, tm, HEAD_SLAB), lambda bi, i, hj: (bi, hj, i, 0)),
                   pl.BlockSpec((None, hb, tm, V_HEAD), lambda bi, i, hj: (bi, hj, i, 0))],
        out_shape=[jax.ShapeDtypeStruct((b, MLA_HEADS, s, HEAD_SLAB), BF16),
                   jax.ShapeDtypeStruct((b, MLA_HEADS, s, V_HEAD), BF16)],
        compiler_params=_cparams(3),
        name="kv_up",
    )(ckv, w_kv, g_nope, kr, kss)


def _flash_kernel(q_ref, k_ref, v_ref, o_ref, m_ref, l_ref, acc_ref, *, tq):
    i = pl.program_id(2)
    q = q_ref[...]
    m_ref[...] = jnp.full_like(m_ref, -jnp.inf)
    l_ref[...] = jnp.zeros_like(l_ref)
    acc_ref[...] = jnp.zeros_like(acc_ref)

    def block(kb, diagonal):
        rows = pl.ds(pl.multiple_of(kb * tq, tq), tq)
        s = _dot_nt(q, k_ref[rows, :])
        if diagonal:
            qi = lax.broadcasted_iota(I32, s.shape, 0)
            ki = lax.broadcasted_iota(I32, s.shape, 1)
            s = jnp.where(ki <= qi, s, -jnp.inf)
        m_prev = m_ref[...]
        m_new = jnp.maximum(m_prev, jnp.max(s, axis=-1, keepdims=True))
        p = jnp.exp(s - m_new)
        alpha = jnp.exp(m_prev - m_new)
        l_ref[...] = alpha * l_ref[...] + jnp.sum(p, axis=-1, keepdims=True)
        acc_ref[...] = alpha * acc_ref[...] + _dot(p.astype(BF16), v_ref[rows, :])
        m_ref[...] = m_new

    def body(kb, c):
        block(kb, False)
        return c

    lax.fori_loop(0, i, body, 0)
    block(i, True)
    o_ref[...] = (acc_ref[...] / l_ref[...]).astype(o_ref.dtype)


def _flash(q, k, v, tq=512):
    b, h, s, _ = q.shape
    return pl.pallas_call(
        functools.partial(_flash_kernel, tq=tq),
        grid=(b, h, s // tq),
        in_specs=[pl.BlockSpec((None, None, tq, HEAD_SLAB), lambda bi, hi, i: (bi, hi, i, 0)),
                  pl.BlockSpec((None, None, s, HEAD_SLAB), lambda bi, hi, i: (bi, hi, 0, 0)),
                  pl.BlockSpec((None, None, s, V_HEAD), lambda bi, hi, i: (bi, hi, 0, 0))],
        out_specs=pl.BlockSpec((None, tq, V_HEAD), lambda bi, hi, i: (bi, i, hi)),
        out_shape=jax.ShapeDtypeStruct((b, s, h * V_HEAD), BF16),
        scratch_shapes=[pltpu.VMEM((tq, 1), F32), pltpu.VMEM((tq, 1), F32), pltpu.VMEM((tq, V_HEAD), F32)],
        compiler_params=_cparams(3),
        name="flash",
    )(q, k, v)


def _q_slab_weights(w_q_up):
    r = w_q_up.shape[0]
    w = w_q_up.reshape(r, MLA_HEADS, QK_HEAD)
    w = jnp.concatenate([w, jnp.zeros((r, MLA_HEADS, HEAD_SLAB - QK_HEAD), w.dtype)], axis=2)
    return w.reshape(r, MLA_HEADS * HEAD_SLAB).astype(BF16)


def _pad_lanes(x, width):
    return jnp.concatenate([x, jnp.zeros((width - x.shape[0],), x.dtype)]).reshape(1, width)


def kernel(x, c, positions, l0_mix_norm, l0_mix_ada_w, l0_mix_ada_b, l0_in_proj, l0_conv_a, l0_conv_b, l0_conv_b_bias, l0_conv_b_ln_g, l0_conv_b_ln_b, l0_out_proj, l0_ffn_norm, l0_ffn_ada_w, l0_ffn_ada_b, l0_exp_gate, l0_exp_up, l0_exp_down, l1_mix_norm, l1_mix_ada_w, l1_mix_ada_b, l1_in_proj, l1_q_a_norm, l1_kv_a_norm, l1_q_up, l1_kv_up, l1_q_norm, l1_k_norm, l1_out_proj, l1_ffn_norm, l1_ffn_ada_w, l1_ffn_ada_b, l1_exp_gate, l1_exp_up, l1_exp_down, router_w, router_b):
    b, s, d = x.shape
    t = b * s
    c8 = jnp.concatenate([c, jnp.zeros((8 - b, d), c.dtype)], axis=0)
    conv_c = l0_conv_a.shape[1]
    q_rank = l1_q_a_norm.shape[0]
    kv_rank = l1_kv_a_norm.shape[0]

    shift, scale, gate = _split_ada(_ada(c8, l0_mix_ada_w, l0_mix_ada_b), b, d)
    u = _norm_mod(x, l0_mix_norm, scale, shift, BF16)
    a_bg, v, ug = _l0_in(u.reshape(t, d), l0_in_proj.astype(BF16), conv_c)
    ycat = _l0_conv(a_bg.reshape(b, s, conv_c), v.reshape(b, s, conv_c), ug.reshape(b, s, conv_c),
                    l0_conv_a, l0_conv_b, l0_conv_b_bias, l0_conv_b_ln_g, l0_conv_b_ln_b)
    h = _out_res(ycat, l0_out_proj.astype(BF16), x, gate)

    shift, scale, gate = _split_ada(_ada(c8, l0_ffn_ada_w, l0_ffn_ada_b), b, d)
    h = _moe(h, l0_ffn_norm, scale, shift, gate, router_w, router_b,
             l0_exp_gate.astype(BF16), l0_exp_up.astype(BF16), l0_exp_down.astype(BF16))

    shift, scale, gate = _split_ada(_ada(c8, l1_mix_ada_w, l1_mix_ada_b), b, d)
    u = _norm_mod(h, l1_mix_norm, scale, shift, BF16)
    tabs = _rope_tables(positions)
    n_in = l1_in_proj.shape[1]
    w_in_pad = jnp.concatenate(
        [l1_in_proj, jnp.zeros((d, q_rank + kv_rank + LANES - n_in), l1_in_proj.dtype)], axis=1).astype(BF16)
    gk_pe = _pad_lanes(l1_k_norm[QK_NOPE:], LANES)
    cq, ckv, kr, kss = _mla_in(u.reshape(t, d), w_in_pad, l1_q_a_norm, l1_kv_a_norm, gk_pe, tabs, q_rank, kv_rank)
    gq_slab = _pad_lanes(l1_q_norm, HEAD_SLAB) * (QK_HEAD ** -0.5)
    q = _q_up(cq, _q_slab_weights(l1_q_up), gq_slab, tabs, b, s)
    k, vv = _kv_up(ckv, l1_kv_up.astype(BF16), l1_k_norm[:QK_NOPE].reshape(1, QK_NOPE), kr, kss, b, s)
    o = _flash(q, k, vv)
    h = _out_res(o, l1_out_proj.astype(BF16), h, gate)

    shift, scale, gate = _split_ada(_ada(c8, l1_ffn_ada_w, l1_ffn_ada_b), b, d)
    h = _moe(h, l1_ffn_norm, scale, shift, gate, router_w, router_b,
             l1_exp_gate.astype(BF16), l1_exp_up.astype(BF16), l1_exp_down.astype(BF16))
    return h
```

```python
import functools

import jax
import jax.numpy as jnp
from jax import lax
from jax.experimental import pallas as pl
from jax.experimental.pallas import tpu as pltpu

F32 = jnp.float32
BF16 = jnp.bfloat16
I32 = jnp.int32

EPS = 1e-6
ROPE_THETA = 10000.0

N_EXPERTS = 16
N_GROUPS = 4
GROUP_SIZE = N_EXPERTS // N_GROUPS
CONV_A_WIDTH = 3
CONV_B_WIDTH = 31
CONV_HALO = 32
MLA_HEADS = 32
QK_NOPE = 128
QK_ROPE = 64
QK_HEAD = QK_NOPE + QK_ROPE
V_HEAD = 128
HEAD_SLAB = 256
LANES = 128
FLASH_TILE = 512
LOG2_E = 1.4426950408889634

MOE_TILE = 512
ROW_CHUNK = 32
DMA_UNROLL = 8
VMEM_LIMIT = 56 * 1024 * 1024


def _cparams(n_axes, vmem=VMEM_LIMIT):
    return pltpu.CompilerParams(dimension_semantics=("arbitrary",) * n_axes, vmem_limit_bytes=vmem)


def _dot(a, b):
    return jnp.dot(a, b, preferred_element_type=F32)


def _dot_nt(a, b):
    return lax.dot_general(a, b, (((1,), (1,)), ((), ())), preferred_element_type=F32)


def _sigmoid(x):
    return 1.0 / (1.0 + jnp.exp(-x))


def _ada_kernel(c_ref, w_ref, b_ref, o_ref):
    c = c_ref[...]
    cond = (c * _sigmoid(c)).astype(BF16)
    o_ref[...] = _dot(cond, w_ref[...].astype(BF16)) + b_ref[...]


def _ada(c8, w, b):
    d, n = w.shape
    tn = 512
    return pl.pallas_call(
        _ada_kernel,
        grid=(n // tn,),
        in_specs=[pl.BlockSpec((8, d), lambda j: (0, 0)),
                  pl.BlockSpec((d, tn), lambda j: (0, j)),
                  pl.BlockSpec((1, tn), lambda j: (0, j))],
        out_specs=pl.BlockSpec((8, tn), lambda j: (0, j)),
        out_shape=jax.ShapeDtypeStruct((8, n), F32),
        compiler_params=_cparams(1),
        name="ada",
    )(c8, w, b.reshape(1, n))


def _split_ada(m, batch, d):
    m = m[:batch]
    return (m[:, None, 0:d], m[:, None, d:2 * d], m[:, None, 2 * d:3 * d])


def _modulated_norm(h, g, scale, shift):
    inv = lax.rsqrt(jnp.mean(h * h, axis=-1, keepdims=True) + EPS)
    return (h * inv) * (g * (1.0 + scale)) + shift


def _norm_mod_kernel(h_ref, g_ref, sc_ref, sh_ref, o_ref):
    o_ref[...] = _modulated_norm(h_ref[...], g_ref[...], sc_ref[...], sh_ref[...]).astype(o_ref.dtype)


def _norm_mod(h, g, scale, shift, out_dtype, tm=256):
    b, s, d = h.shape
    return pl.pallas_call(
        _norm_mod_kernel,
        grid=(b, s // tm),
        in_specs=[pl.BlockSpec((None, tm, d), lambda bi, i: (bi, i, 0)),
                  pl.BlockSpec((1, d), lambda bi, i: (0, 0)),
                  pl.BlockSpec((None, 1, d), lambda bi, i: (bi, 0, 0)),
                  pl.BlockSpec((None, 1, d), lambda bi, i: (bi, 0, 0))],
        out_specs=pl.BlockSpec((None, tm, d), lambda bi, i: (bi, i, 0)),
        out_shape=jax.ShapeDtypeStruct((b, s, d), out_dtype),
        compiler_params=_cparams(2),
        name="norm_mod",
    )(h, g.reshape(1, d), scale, shift)


def _l0_in_kernel(u_ref, w_bg, w_cg, w_x, w_val, w_gate, o_bg, o_v, o_ug):
    u = u_ref[...]
    o_bg[...] = _dot(u, w_bg[...]).astype(BF16)
    o_v[...] = (_dot(u, w_cg[...]) * _dot(u, w_x[...])).astype(BF16)
    o_ug[...] = (_dot(u, w_val[...]) * _sigmoid(_dot(u, w_gate[...]))).astype(BF16)


def _l0_in(u, w_in, c, tm=512, tn=256):
    t, d = u.shape
    nj = c // tn
    w_specs = [pl.BlockSpec((d, tn), functools.partial(lambda i, j, g: (0, g * nj + j), g=g)) for g in range(5)]
    out_spec = pl.BlockSpec((tm, tn), lambda i, j: (i, j))
    out_sds = jax.ShapeDtypeStruct((t, c), BF16)
    return pl.pallas_call(
        _l0_in_kernel,
        grid=(t // tm, nj),
        in_specs=[pl.BlockSpec((tm, d), lambda i, j: (i, 0))] + w_specs,
        out_specs=[out_spec] * 3,
        out_shape=[out_sds] * 3,
        compiler_params=_cparams(2),
        name="l0_in",
    )(u, w_in, w_in, w_in, w_in, w_in)


def _l0_conv_kernel(bg_ref, v_ref, ug_ref, vh_ref, ugh_ref, wa_ref, wb_ref, bb_ref, lg_ref, lb_ref,
                    o_ref, vs_ref, us_ref, ub_ref, *, ts, c, rc):
    first = pl.program_id(1) == 0
    keep = jnp.where(first, 0.0, 1.0)
    vs_ref[0:CONV_HALO, :] = vh_ref[...].astype(F32) * keep
    us_ref[0:CONV_HALO, :] = ugh_ref[...].astype(F32) * keep
    vs_ref[CONV_HALO:, :] = v_ref[...].astype(F32)
    us_ref[CONV_HALO:, :] = ug_ref[...].astype(F32)

    win = rc + CONV_HALO

    def causal_conv(w_ref, width, window, cols, acc):
        for res in range(8):
            taps = [k for k in range(width) if (CONV_HALO - (width - 1) + k) % 8 == res]
            if not taps:
                continue
            shifted = window if res == 0 else pltpu.roll(window, win - res, 0)
            for k in taps:
                a0 = (CONV_HALO - (width - 1) + k) - res
                acc = acc + w_ref[k:k + 1, cols] * shifted[a0:a0 + rc, :]
        return acc

    def rows(r, carry):
        r0 = pl.multiple_of(r * rc, rc)
        for cj in range(c // LANES):
            cols = pl.ds(cj * LANES, LANES)
            acc = causal_conv(wa_ref, CONV_A_WIDTH, vs_ref[pl.ds(r0, win), cols], cols, jnp.zeros((rc, LANES), F32))
            o_ref[pl.ds(r0, rc), cols] = (bg_ref[pl.ds(r0, rc), cols].astype(F32) * acc).astype(BF16)
            acc = causal_conv(wb_ref, CONV_B_WIDTH, us_ref[pl.ds(r0, win), cols], cols,
                              jnp.zeros((rc, LANES), F32) + bb_ref[:, cols])
            ub_ref[pl.ds(r0, rc), cols] = acc
        return carry

    lax.fori_loop(0, ts // rc, rows, 0)

    ub = ub_ref[...]
    mu = jnp.mean(ub, axis=-1, keepdims=True)
    xc = ub - mu
    var = jnp.mean(xc * xc, axis=-1, keepdims=True)
    y = xc * lax.rsqrt(var + EPS) * lg_ref[...] + lb_ref[...]
    o_ref[:, c:] = (y * _sigmoid(y)).astype(BF16)


def _l0_conv(a_bg, v, ug, conv_a, conv_b, conv_b_bias, ln_g, ln_b, ts=256, rc=32):
    b, s, c = v.shape
    hb = ts // CONV_HALO
    main = pl.BlockSpec((None, ts, c), lambda bi, i: (bi, i, 0))
    halo = pl.BlockSpec((None, CONV_HALO, c), lambda bi, i: (bi, jnp.maximum(i * hb - 1, 0), 0))
    wb_rows = 32
    conv_b_pad = jnp.zeros((wb_rows, c), F32).at[:CONV_B_WIDTH].set(conv_b)
    conv_a_pad = jnp.zeros((8, c), F32).at[:CONV_A_WIDTH].set(conv_a)
    full = lambda r: pl.BlockSpec((r, c), lambda bi, i: (0, 0))
    return pl.pallas_call(
        functools.partial(_l0_conv_kernel, ts=ts, c=c, rc=rc),
        grid=(b, s // ts),
        in_specs=[main, main, main, halo, halo, full(8), full(wb_rows), full(1), full(1), full(1)],
        out_specs=pl.BlockSpec((None, ts, 2 * c), lambda bi, i: (bi, i, 0)),
        out_shape=jax.ShapeDtypeStruct((b, s, 2 * c), BF16),
        scratch_shapes=[pltpu.VMEM((CONV_HALO + ts, c), F32),
                        pltpu.VMEM((CONV_HALO + ts, c), F32),
                        pltpu.VMEM((ts, c), F32)],
        compiler_params=_cparams(2),
        name="l0_conv",
    )(a_bg, v, ug, v, ug, conv_a_pad, conv_b_pad, conv_b_bias.reshape(1, c), ln_g.reshape(1, c), ln_b.reshape(1, c))


def _out_res_kernel(a_ref, w_ref, h_ref, gate_ref, o_ref):
    o_ref[...] = h_ref[...] + gate_ref[...] * _dot(a_ref[...], w_ref[...])


def _out_res(a, w, h, gate, tm=1024, tn=512):
    b, s, k = a.shape
    d = w.shape[1]
    return pl.pallas_call(
        _out_res_kernel,
        grid=(b, s // tm, d // tn),
        in_specs=[pl.BlockSpec((None, tm, k), lambda bi, i, j: (bi, i, 0)),
                  pl.BlockSpec((k, tn), lambda bi, i, j: (0, j)),
                  pl.BlockSpec((None, tm, tn), lambda bi, i, j: (bi, i, j)),
                  pl.BlockSpec((None, 1, tn), lambda bi, i, j: (bi, 0, j))],
        out_specs=pl.BlockSpec((None, tm, tn), lambda bi, i, j: (bi, i, j)),
        out_shape=jax.ShapeDtypeStruct((b, s, d), F32),
        compiler_params=_cparams(3),
        name="out_res",
    )(a, w, h, gate)


def _route_kernel(h_ref, g_ref, sc_ref, sh_ref, rw_ref, rb_ref, ri_ref, rwt_ref, cnt_ref,
                  tri_ref, carry_ref, *, tm):
    step0 = jnp.logical_and(pl.program_id(0) == 0, pl.program_id(1) == 0)

    @pl.when(step0)
    def _():
        carry_ref[...] = jnp.zeros_like(carry_ref)
        src = lax.broadcasted_iota(I32, (tm, tm), 0)
        dst = lax.broadcasted_iota(I32, (tm, tm), 1)
        tri_ref[...] = jnp.where(src < dst, 1.0, 0.0).astype(BF16)

    u = _modulated_norm(h_ref[...], g_ref[...], sc_ref[...], sh_ref[...])
    u_hi = u.astype(BF16)
    u_lo = (u - u_hi.astype(F32)).astype(BF16)
    rw = rw_ref[...]
    rw_hi = rw.astype(BF16)
    rw_lo = (rw - rw_hi.astype(F32)).astype(BF16)
    logits = _dot_nt(rw_hi, u_hi) + (_dot_nt(rw_hi, u_lo) + _dot_nt(rw_lo, u_hi))
    aff = _sigmoid(logits)
    sel = aff + rb_ref[...]
    row = lax.broadcasted_iota(I32, (N_EXPERTS, tm), 0)

    best, best_g = None, None
    for g in range(N_GROUPS):
        s = [sel[g * GROUP_SIZE + j:g * GROUP_SIZE + j + 1, :] for j in range(GROUP_SIZE)]
        score = None
        for a in range(GROUP_SIZE):
            for bb in range(a + 1, GROUP_SIZE):
                pair = s[a] + s[bb]
                score = pair if score is None else jnp.maximum(score, pair)
        if best is None:
            best, best_g = score, jnp.zeros((1, tm), I32)
        else:
            upd = score > best
            best = jnp.where(upd, score, best)
            best_g = jnp.where(upd, g, best_g)

    neg = -jnp.inf
    masked = jnp.where((row // GROUP_SIZE) == best_g, sel, neg)
    m1 = jnp.max(masked, axis=0, keepdims=True)
    i1 = jnp.min(jnp.where(masked == m1, row, N_EXPERTS), axis=0, keepdims=True)
    masked2 = jnp.where(row == i1, neg, masked)
    m2 = jnp.max(masked2, axis=0, keepdims=True)
    i2 = jnp.min(jnp.where(masked2 == m2, row, N_EXPERTS), axis=0, keepdims=True)
    hit1 = row == i1
    hit2 = row == i2
    a1 = jnp.sum(jnp.where(hit1, aff, 0.0), axis=0, keepdims=True)
    a2 = jnp.sum(jnp.where(hit2, aff, 0.0), axis=0, keepdims=True)
    den = a1 + a2
    w1 = a1 / den
    w2 = a2 / den

    onehot = jnp.where(jnp.logical_or(hit1, hit2), 1.0, 0.0)
    before = _dot(onehot.astype(BF16), tri_ref[...]) + carry_ref[...]
    r1 = jnp.sum(jnp.where(hit1, before, 0.0), axis=0, keepdims=True).astype(I32)
    r2 = jnp.sum(jnp.where(hit2, before, 0.0), axis=0, keepdims=True).astype(I32)
    carry_ref[...] = carry_ref[...] + jnp.sum(onehot, axis=1, keepdims=True)

    zi = jnp.zeros((4, tm), I32)
    ri_ref[...] = jnp.concatenate([i1, i2, r1, r2, zi], axis=0)
    rwt_ref[...] = jnp.concatenate([w1, w2, jnp.zeros((6, tm), F32)], axis=0)
    cnt_ref[...] = jnp.broadcast_to(carry_ref[...], cnt_ref.shape)


def _route(h, g, scale, shift, router_w, router_b, tm=512):
    b, s, d = h.shape
    t = b * s
    n_i = s // tm
    return pl.pallas_call(
        functools.partial(_route_kernel, tm=tm),
        grid=(b, n_i),
        in_specs=[pl.BlockSpec((None, tm, d), lambda bi, i: (bi, i, 0)),
                  pl.BlockSpec((1, d), lambda bi, i: (0, 0)),
                  pl.BlockSpec((None, 1, d), lambda bi, i: (bi, 0, 0)),
                  pl.BlockSpec((None, 1, d), lambda bi, i: (bi, 0, 0)),
                  pl.BlockSpec((N_EXPERTS, d), lambda bi, i: (0, 0)),
                  pl.BlockSpec((N_EXPERTS, 1), lambda bi, i: (0, 0))],
        out_specs=[pl.BlockSpec((8, tm), lambda bi, i: (0, bi * n_i + i)),
                   pl.BlockSpec((8, tm), lambda bi, i: (0, bi * n_i + i)),
                   pl.BlockSpec((N_EXPERTS, LANES), lambda bi, i: (0, 0))],
        out_shape=[jax.ShapeDtypeStruct((8, t), I32),
                   jax.ShapeDtypeStruct((8, t), F32),
                   jax.ShapeDtypeStruct((N_EXPERTS, LANES), F32)],
        scratch_shapes=[pltpu.VMEM((tm, tm), BF16), pltpu.VMEM((N_EXPERTS, 1), F32)],
        compiler_params=_cparams(2),
        name="route",
    )(h, g.reshape(1, d), scale, shift, router_w.T, router_b.reshape(N_EXPERTS, 1))


def _pack_bf16_pairs(u):
    half = u.shape[1] // 2
    lo = lax.bitcast_convert_type(u[:, :half].astype(BF16).astype(F32), jnp.uint32)
    hi = lax.bitcast_convert_type(u[:, half:].astype(BF16).astype(F32), jnp.uint32)
    return (lo >> 16) | (hi & jnp.uint32(0xFFFF0000))


def _unpack_bf16_pairs(w):
    lo = lax.bitcast_convert_type(w << 16, F32).astype(BF16)
    hi = lax.bitcast_convert_type(w & jnp.uint32(0xFFFF0000), F32).astype(BF16)
    return jnp.concatenate([lo, hi], axis=1)


def _dispatch_kernel(off_ref, cnt_ref, nt_ref, ri_ref, h_ref, g_ref, sc_ref, sh_ref, xs_hbm,
                     xp_ref, zero_ref, sem, *, tb):
    step0 = jnp.logical_and(pl.program_id(0) == 0, pl.program_id(1) == 0)
    zrows = zero_ref.shape[0]

    def row_copy(tt, pos, k):
        return pltpu.make_async_copy(xp_ref.at[pl.ds(tt, 1)], xs_hbm.at[pl.ds(pos, 1)], sem.at[k])

    def pad_copy(pos):
        return pltpu.make_async_copy(zero_ref.at[pl.ds(0, 1)], xs_hbm.at[pl.ds(pos, 1)], sem.at[2])

    def tail_copy(blk):
        start = pl.multiple_of(nt_ref[0] * MOE_TILE + blk * zrows, zrows)
        return pltpu.make_async_copy(zero_ref, xs_hbm.at[pl.ds(start, zrows)], sem.at[2])

    @pl.when(step0)
    def _():
        zero_ref[...] = jnp.zeros_like(zero_ref)
        n_tail =(xs_hbm.shape[0] // MOE_TILE - nt_ref[0]) * (MOE_TILE // zrows)
        lax.fori_loop(0, n_tail, lambda blk, c: (tail_copy(blk).start(), c)[1], 0)
        lax.fori_loop(0, n_tail, lambda blk, c: (tail_copy(blk).wait(), c)[1], 0)
        for e in range(N_EXPERTS):
            n_pad = (MOE_TILE - (cnt_ref[e] & (MOE_TILE - 1))) & (MOE_TILE - 1)
            base = off_ref[e] + cnt_ref[e]

            def start(r, c, base=base):
                pad_copy(base + r).start()
                return c

            def wait(r, c, base=base):
                pad_copy(base + r).wait()
                return c

            lax.fori_loop(0, n_pad, start, 0)
            lax.fori_loop(0, n_pad, wait, 0)

    def chunk(r, c):
        rs = pl.ds(pl.multiple_of(r * ROW_CHUNK, ROW_CHUNK), ROW_CHUNK)
        xp_ref[rs, :] = _pack_bf16_pairs(_modulated_norm(h_ref[rs, :], g_ref[...], sc_ref[...], sh_ref[...]))
        return c

    lax.fori_loop(0, tb // ROW_CHUNK, chunk, 0)

    def start(tt, c):
        for k in range(2):
            pos = off_ref[ri_ref[k, tt]] + ri_ref[2 + k, tt]
            row_copy(tt, pos, k).start()
        return c

    def wait(tt, c):
        for k in range(2):
            row_copy(0, 0, k).wait()
        return c

    lax.fori_loop(0, tb, start, 0, unroll=DMA_UNROLL)
    lax.fori_loop(0, tb, wait, 0, unroll=DMA_UNROLL)


def _dispatch(h, g, scale, shift, ri, off, counts, n_tiles, n_rows, tb=256):
    b, s, d = h.shape
    n_i = s // tb
    grid_spec = pltpu.PrefetchScalarGridSpec(
        num_scalar_prefetch=3,
        grid=(b, n_i),
        in_specs=[pl.BlockSpec((8, tb), lambda bi, i, off, cnt, nt: (0, bi * n_i + i), memory_space=pltpu.SMEM),
                  pl.BlockSpec((None, tb, d), lambda bi, i, off, cnt, nt: (bi, i, 0)),
                  pl.BlockSpec((1, d), lambda bi, i, off, cnt, nt: (0, 0)),
                  pl.BlockSpec((None, 1, d), lambda bi, i, off, cnt, nt: (bi, 0, 0)),
                  pl.BlockSpec((None, 1, d), lambda bi, i, off, cnt, nt: (bi, 0, 0))],
        out_specs=pl.BlockSpec(memory_space=pl.ANY),
        scratch_shapes=[pltpu.VMEM((tb, d // 2), jnp.uint32), pltpu.VMEM((64, d // 2), jnp.uint32),
                        pltpu.SemaphoreType.DMA((3,))],
    )
    return pl.pallas_call(
        functools.partial(_dispatch_kernel, tb=tb),
        grid_spec=grid_spec,
        out_shape=jax.ShapeDtypeStruct((n_rows, d // 2), jnp.uint32),
        compiler_params=_cparams(2),
        name="dispatch",
    )(off, counts, n_tiles, ri, h, g.reshape(1, d), scale, shift)


def _experts_kernel(te_ref, nt_ref, xs_ref, wg_ref, wu_ref, wd_ref, ys_ref, xb_ref):
    j = pl.program_id(0)
    cidx = pl.program_id(1)

    @pl.when(j < nt_ref[0])
    def _():
        @pl.when(cidx == 0)
        def _():
            xb_ref[...] = _unpack_bf16_pairs(xs_ref[...])

        x = xb_ref[...]
        gate = _dot(x, wg_ref[...])
        up = _dot(x, wu_ref[...])
        he = (gate * _sigmoid(gate) * up).astype(BF16)
        d = ys_ref.shape[1]
        dn = min(d, 1024)
        for dj in range(d // dn):
            cols = pl.ds(dj * dn, dn)
            y = _dot(he, wd_ref[:, cols])

            @pl.when(cidx == 0)
            def _():
                ys_ref[:, cols] = y

            @pl.when(cidx != 0)
            def _():
                ys_ref[:, cols] = ys_ref[:, cols] + y

    @pl.when(jnp.logical_and(j >= nt_ref[0], cidx == 0))
    def _():
        ys_ref[...] = jnp.zeros_like(ys_ref)


def _experts(xs, w_gate, w_up, w_down, tile_expert, n_tiles, tf=256):
    n_rows = xs.shape[0]
    d, f = w_gate.shape[1:]
    nt_max = n_rows // MOE_TILE
    nc = f // tf

    def tile(j, c, te, nt):
        return jnp.minimum(j, nt[0] - 1)

    def chunk(j, c, te, nt):
        return jnp.where(j < nt[0], c, nc - 1)

    grid_spec = pltpu.PrefetchScalarGridSpec(
        num_scalar_prefetch=2,
        grid=(nt_max, nc),
        in_specs=[pl.BlockSpec((MOE_TILE, d // 2), lambda j, c, te, nt: (tile(j, c, te, nt), 0)),
                  pl.BlockSpec((None, d, tf), lambda j, c, te, nt: (te[tile(j, c, te, nt)], 0, chunk(j, c, te, nt))),
                  pl.BlockSpec((None, d, tf), lambda j, c, te, nt: (te[tile(j, c, te, nt)], 0, chunk(j, c, te, nt))),
                  pl.BlockSpec((None, tf, d), lambda j, c, te, nt: (te[tile(j, c, te, nt)], chunk(j, c, te, nt), 0))],
        out_specs=pl.BlockSpec((MOE_TILE, d), lambda j, c, te, nt: (j, 0)),
        scratch_shapes=[pltpu.VMEM((MOE_TILE, d), BF16)],
    )
    return pl.pallas_call(
        _experts_kernel,
        grid_spec=grid_spec,
        out_shape=jax.ShapeDtypeStruct((n_rows, d), F32),
        compiler_params=_cparams(2),
        name="experts",
    )(tile_expert, n_tiles, xs, w_gate, w_up, w_down)


def _combine_kernel(off_ref, ri_ref, w_ref, h_ref, gate_ref, ys_hbm, o_ref, y0_ref, y1_ref, sem, *, tc):
    bufs = (y0_ref, y1_ref)

    def row_copy(pos, tt, k):
        return pltpu.make_async_copy(ys_hbm.at[pl.ds(pos, 1)], bufs[k].at[pl.ds(tt, 1)], sem.at[k])

    def start(tt, c):
        for k in range(2):
            pos = off_ref[ri_ref[k, tt]] + ri_ref[2 + k, tt]
            row_copy(pos, tt, k).start()
        return c

    def wait(tt, c):
        for k in range(2):
            row_copy(0, 0, k).wait()
        return c

    lax.fori_loop(0, tc, start, 0, unroll=DMA_UNROLL)
    lax.fori_loop(0, tc, wait, 0, unroll=DMA_UNROLL)

    def chunk(r, c):
        rs = pl.ds(pl.multiple_of(r * ROW_CHUNK, ROW_CHUNK), ROW_CHUNK)
        w = w_ref[rs, :]
        y = w[:, 0:1] * y0_ref[rs, :] + w[:, 1:2] * y1_ref[rs, :]
        o_ref[rs, :] = h_ref[rs, :] + gate_ref[...] * y
        return c

    lax.fori_loop(0, tc // ROW_CHUNK, chunk, 0)


def _combine(h, gate, ys, ri, wts, off, tc=256):
    b, s, d = h.shape
    n_i = s // tc
    grid_spec = pltpu.PrefetchScalarGridSpec(
        num_scalar_prefetch=1,
        grid=(b, n_i),
        in_specs=[pl.BlockSpec((8, tc), lambda bi, i, off: (0, bi * n_i + i), memory_space=pltpu.SMEM),
                  pl.BlockSpec((tc, 2), lambda bi, i, off: (bi * n_i + i, 0)),
                  pl.BlockSpec((None, tc, d), lambda bi, i, off: (bi, i, 0)),
                  pl.BlockSpec((None, 1, d), lambda bi, i, off: (bi, 0, 0)),
                  pl.BlockSpec(memory_space=pl.ANY)],
        out_specs=pl.BlockSpec((None, tc, d), lambda bi, i, off: (bi, i, 0)),
        scratch_shapes=[pltpu.VMEM((tc, d), F32), pltpu.VMEM((tc, d), F32), pltpu.SemaphoreType.DMA((2,))],
    )
    return pl.pallas_call(
        functools.partial(_combine_kernel, tc=tc),
        grid_spec=grid_spec,
        out_shape=jax.ShapeDtypeStruct((b, s, d), F32),
        compiler_params=_cparams(2),
        name="combine",
    )(off, ri, wts, h, gate, ys)


def _moe(h, g, scale, shift, gate, router_w, router_b, w_gate, w_up, w_down):
    b, s, d = h.shape
    t = b * s
    nt_max = (2 * t) // MOE_TILE + N_EXPERTS
    ri, rwt, cnt = _route(h, g, scale, shift, router_w, router_b)
    counts = cnt[:, 0].astype(I32)
    tiles_per = (counts + MOE_TILE - 1) // MOE_TILE
    tile_end = jnp.cumsum(tiles_per)
    off = (tile_end - tiles_per) * MOE_TILE
    tile_expert = jnp.minimum(
        jnp.searchsorted(tile_end, jnp.arange(nt_max, dtype=I32), side="right"), N_EXPERTS - 1).astype(I32)
    n_tiles = tile_end[-1:].astype(I32)
    xs = _dispatch(h, g, scale, shift, ri, off, counts, n_tiles, nt_max * MOE_TILE)
    ys = _experts(xs, w_gate, w_up, w_down, tile_expert, n_tiles)
    return _combine(h, gate, ys, ri, rwt[0:2].T, off)


def _rope_tables_kernel(pos_ref, invf_ref, c_ref, s1_ref, s2_ref):
    ang = pos_ref[...].astype(F32) * invf_ref[...]
    lane = lax.broadcasted_iota(I32, ang.shape, 1)
    cos = jnp.cos(ang)
    sin = jnp.sin(ang)
    half = QK_ROPE // 2
    c_ref[...] = jnp.where(lane < QK_ROPE, cos, 0.0)
    s1_ref[...] = jnp.where(lane < half, -sin, 0.0)
    s2_ref[...] = jnp.where(jnp.logical_and(lane >= half, lane < QK_ROPE), sin, 0.0)


def _rope_tables(positions, tm=1024):
    t = positions.size
    half = QK_ROPE // 2
    inv_freq = 1.0 / (ROPE_THETA ** (jnp.arange(0, QK_ROPE, 2, dtype=F32) / QK_ROPE))
    invf = jnp.concatenate([inv_freq, inv_freq, jnp.zeros((LANES - 2 * half,), F32)]).reshape(1, LANES)
    spec = pl.BlockSpec((tm, LANES), lambda i: (i, 0))
    sds = jax.ShapeDtypeStruct((t, LANES), F32)
    return pl.pallas_call(
        _rope_tables_kernel,
        grid=(t // tm,),
        in_specs=[pl.BlockSpec((tm, 1), lambda i: (i, 0)), pl.BlockSpec((1, LANES), lambda i: (0, 0))],
        out_specs=[spec] * 3,
        out_shape=[sds] * 3,
        compiler_params=_cparams(1),
        name="rope_tables",
    )(positions.reshape(t, 1), invf)


def _rope(p, c, s1, s2):
    return p * c + pltpu.roll(p, 96, 1) * s1 + pltpu.roll(p, 32, 1) * s2


def _mla_in_kernel(u_ref, w_ref, gq_ref, gkv_ref, gkpe_ref, c_ref, s1_ref, s2_ref,
                   cq_ref, ckv_ref, kr_ref, kss_ref, *, q_rank, kv_rank):
    proj = _dot(u_ref[...], w_ref[...])
    cq = proj[:, :q_rank]
    cq_ref[...] = (cq * lax.rsqrt(jnp.mean(cq * cq, axis=-1, keepdims=True) + EPS) * gq_ref[...]).astype(BF16)
    ckv = proj[:, q_rank:q_rank + kv_rank]
    ckv_ref[...] = (ckv * lax.rsqrt(jnp.mean(ckv * ckv, axis=-1, keepdims=True) + EPS) * gkv_ref[...]).astype(BF16)
    kpe = proj[:, q_rank + kv_rank:]
    kss_ref[...] = jnp.broadcast_to(jnp.sum(kpe * kpe, axis=-1, keepdims=True), kss_ref.shape)
    kr_ref[...] = _rope(kpe * gkpe_ref[...], c_ref[...], s1_ref[...], s2_ref[...])


def _mla_in(u, w_in_pad, gq, gkv, gkpe, tabs, q_rank, kv_rank, tm=512):
    t, d = u.shape
    n = w_in_pad.shape[1]
    row = lambda w: pl.BlockSpec((tm, w), lambda i: (i, 0))
    const = lambda w: pl.BlockSpec((1, w), lambda i: (0, 0))
    return pl.pallas_call(
        functools.partial(_mla_in_kernel, q_rank=q_rank, kv_rank=kv_rank),
        grid=(t // tm,),
        in_specs=[row(d), pl.BlockSpec((d, n), lambda i: (0, 0)), const(q_rank), const(kv_rank), const(LANES),
                  row(LANES), row(LANES), row(LANES)],
        out_specs=[row(q_rank), row(kv_rank), row(LANES), row(LANES)],
        out_shape=[jax.ShapeDtypeStruct((t, q_rank), BF16), jax.ShapeDtypeStruct((t, kv_rank), BF16),
                   jax.ShapeDtypeStruct((t, LANES), F32), jax.ShapeDtypeStruct((t, LANES), F32)],
        compiler_params=_cparams(1),
        name="mla_in",
    )(u, w_in_pad, gq.reshape(1, -1), gkv.reshape(1, -1), gkpe, *tabs)


def _q_up_kernel(cq_ref, w_ref, g_ref, c_ref, s1_ref, s2_ref, q_ref, *, hb):
    cq = cq_ref[...]
    g = g_ref[...]
    for hh in range(hb):
        slab = _dot(cq, w_ref[:, hh * HEAD_SLAB:(hh + 1) * HEAD_SLAB])
        inv = lax.rsqrt(jnp.sum(slab * slab, axis=-1, keepdims=True) * (1.0 / QK_HEAD) + EPS)
        y = slab * inv * g
        rot = _rope(y[:, QK_NOPE:], c_ref[...], s1_ref[...], s2_ref[...])
        q_ref[hh] = jnp.concatenate([y[:, :QK_NOPE], rot], axis=1).astype(BF16)


def _q_up(cq, w_q_slab, g_slab, tabs, b, s, tm=1024, hb=8):
    t, r = cq.shape
    n_i = s // tm
    row = lambda w: pl.BlockSpec((tm, w), lambda bi, i, hj: (bi * n_i + i, 0))
    return pl.pallas_call(
        functools.partial(_q_up_kernel, hb=hb),
        grid=(b, n_i, MLA_HEADS // hb),
        in_specs=[row(r), pl.BlockSpec((r, hb * HEAD_SLAB), lambda bi, i, hj: (0, hj)),
                  pl.BlockSpec((1, HEAD_SLAB), lambda bi, i, hj: (0, 0)), row(LANES), row(LANES), row(LANES)],
        out_specs=pl.BlockSpec((None, hb, tm, HEAD_SLAB), lambda bi, i, hj: (bi, hj, i, 0)),
        out_shape=jax.ShapeDtypeStruct((b, MLA_HEADS, s, HEAD_SLAB), BF16),
        compiler_params=_cparams(3),
        name="q_up",
    )(cq, w_q_slab, g_slab, *tabs)


def _kv_up_kernel(ckv_ref, w_ref, g_ref, kr_ref, kss_ref, k_ref, vt_ref, *, hb):
    ckv = ckv_ref[...]
    g = g_ref[...]
    kr = kr_ref[...]
    kss = kss_ref[:, 0:1]
    n_blk, _, tk = vt_ref.shape[1:]
    for hh in range(hb):
        slab = _dot(ckv, w_ref[:, hh * HEAD_SLAB:(hh + 1) * HEAD_SLAB])
        kn = slab[:, :QK_NOPE]
        inv = lax.rsqrt((jnp.sum(kn * kn, axis=-1, keepdims=True) + kss) * (1.0 / QK_HEAD) + EPS)
        k_ref[hh] = jnp.concatenate([kn * inv * g, kr * inv], axis=1).astype(BF16)
        vt = slab[:, QK_NOPE:].T
        for blk in range(n_blk):
            vt_ref[hh, blk] = vt[:, blk * tk:(blk + 1) * tk].astype(BF16)


def _kv_up(ckv, w_kv, g_nope, kr, kss, b, s, tm=1024, hb=8):
    t, r = ckv.shape
    n_i = s // tm
    n_blk = tm // FLASH_TILE
    row = lambda w: pl.BlockSpec((tm, w), lambda bi, i, hj: (bi * n_i + i, 0))
    return pl.pallas_call(
        functools.partial(_kv_up_kernel, hb=hb),
        grid=(b, n_i, MLA_HEADS // hb),
        in_specs=[row(r), pl.BlockSpec((r, hb * HEAD_SLAB), lambda bi, i, hj: (0, hj)),
                  pl.BlockSpec((1, QK_NOPE), lambda bi, i, hj: (0, 0)), row(LANES), row(LANES)],
        out_specs=[pl.BlockSpec((None, hb, tm, HEAD_SLAB), lambda bi, i, hj: (bi, hj, i, 0)),
                   pl.BlockSpec((None, hb, n_blk, V_HEAD, FLASH_TILE), lambda bi, i, hj: (bi, hj, i, 0, 0))],
        out_shape=[jax.ShapeDtypeStruct((b, MLA_HEADS, s, HEAD_SLAB), BF16),
                   jax.ShapeDtypeStruct((b, MLA_HEADS, s // FLASH_TILE, V_HEAD, FLASH_TILE), BF16)],
        compiler_params=_cparams(3),
        name="kv_up",
    )(ckv, w_kv, g_nope, kr, kss)


def _flash_kernel(q_ref, k_ref, vt_ref, o_ref, m_ref, l_ref, acc_ref, s_buf, p_buf, a_buf, *, nq):
    tq = FLASH_TILE
    m_ref[...] = jnp.full_like(m_ref, -jnp.inf)
    l_ref[...] = jnp.zeros_like(l_ref)
    acc_ref[...] = jnp.zeros_like(acc_ref)

    def rows(j):
        return pl.ds(pl.multiple_of(j * tq, tq), tq)

    def stage_a(qi, kj, slot):
        s_buf[slot] = _dot_nt(k_ref[rows(kj), :], q_ref[rows(qi), :])

    def stage_b(qi, slot, diagonal=False):
        st = s_buf[slot]
        if diagonal:
            key = lax.broadcasted_iota(I32, st.shape, 0)
            qry = lax.broadcasted_iota(I32, st.shape, 1)
            st = jnp.where(key <= qry, st, -jnp.inf)
        m_prev = m_ref[qi]
        m_new = jnp.maximum(m_prev, jnp.max(st, axis=0, keepdims=True))
        p = jnp.exp2(st - m_new)
        alpha = jnp.exp2(m_prev - m_new)
        l_ref[qi] = alpha * l_ref[qi] + jnp.sum(p, axis=0, keepdims=True)
        m_ref[qi] = m_new
        a_buf[slot] = alpha
        p_buf[slot] = p.astype(BF16)

    def stage_c(qi, kj, slot):
        acc_ref[qi] = a_buf[slot] * acc_ref[qi] + _dot(vt_ref[kj], p_buf[slot])

    def advance(qi, kj):
        wrap = kj + 1 == qi
        return jnp.where(wrap, qi + 1, qi), jnp.where(wrap, 0, kj + 1)

    def steps(c, count):
        for u in range(count):
            qa, ka, qb, kb, qc, kc = c
            stage_a(qa, ka, u % 2)
            stage_b(qb, 1 - u % 2)
            stage_c(qc, kc, u % 2)
            c = (*advance(qa, ka), qa, ka, qb, kb)
        return c

    n_off = nq * (nq - 1) // 2
    first = (jnp.int32(1), jnp.int32(0))
    second = advance(*first)
    stage_a(*first, 0)
    stage_a(*second, 1)
    stage_b(first[0], 0)
    c = (*advance(*second), *second, *first)
    c = lax.fori_loop(0, (n_off - 2) // 2, lambda t, c: steps(c, 2), c)
    _, _, qb, kb, qc, kc = c
    stage_b(qb, 1)
    stage_c(qc, kc, 0)
    stage_c(qb, kb, 1)

    def diagonal(qi, c):
        stage_a(qi, qi, 0)
        stage_b(qi, 0, diagonal=True)
        stage_c(qi, qi, 0)
        return c

    lax.fori_loop(0, nq, diagonal, 0)

    def finish(qi, c):
        o_ref[rows(qi), :] = (acc_ref[qi] / l_ref[qi]).T.astype(o_ref.dtype)
        return c

    lax.fori_loop(0, nq, finish, 0)


def _flash(q, k, vt):
    b, h, s, _ = q.shape
    tq = FLASH_TILE
    nq = s // tq
    assert nq >= 3 and (nq * (nq - 1) // 2) % 2 == 0, "the pipeline needs an even number (>= 2) of off-diagonal blocks"
    return pl.pallas_call(
        functools.partial(_flash_kernel, nq=nq),
        grid=(b, h),
        in_specs=[pl.BlockSpec((None, None, s, HEAD_SLAB), lambda bi, hi: (bi, hi, 0, 0)),
                  pl.BlockSpec((None, None, s, HEAD_SLAB), lambda bi, hi: (bi, hi, 0, 0)),
                  pl.BlockSpec((None, None, nq, V_HEAD, tq), lambda bi, hi: (bi, hi, 0, 0, 0))],
        out_specs=pl.BlockSpec((None, s, V_HEAD), lambda bi, hi: (bi, 0, hi)),
        out_shape=jax.ShapeDtypeStruct((b, s, h * V_HEAD), BF16),
        scratch_shapes=[pltpu.VMEM((nq, 1, tq), F32), pltpu.VMEM((nq, 1, tq), F32), pltpu.VMEM((nq, V_HEAD, tq), F32),
                        pltpu.VMEM((2, tq, tq), F32), pltpu.VMEM((2, tq, tq), BF16), pltpu.VMEM((2, 1, tq), F32)],
        compiler_params=_cparams(2),
        name="flash",
    )(q, k, vt)


def _q_slab_weights(w_q_up):
    r = w_q_up.shape[0]
    w = w_q_up.reshape(r, MLA_HEADS, QK_HEAD)
    w = jnp.concatenate([w, jnp.zeros((r, MLA_HEADS, HEAD_SLAB - QK_HEAD), w.dtype)], axis=2)
    return w.reshape(r, MLA_HEADS * HEAD_SLAB).astype(BF16)


def _pad_lanes(x, width):
    return jnp.concatenate([x, jnp.zeros((width - x.shape[0],), x.dtype)]).reshape(1, width)


def kernel(x, c, positions, l0_mix_norm, l0_mix_ada_w, l0_mix_ada_b, l0_in_proj, l0_conv_a, l0_conv_b, l0_conv_b_bias, l0_conv_b_ln_g, l0_conv_b_ln_b, l0_out_proj, l0_ffn_norm, l0_ffn_ada_w, l0_ffn_ada_b, l0_exp_gate, l0_exp_up, l0_exp_down, l1_mix_norm, l1_mix_ada_w, l1_mix_ada_b, l1_in_proj, l1_q_a_norm, l1_kv_a_norm, l1_q_up, l1_kv_up, l1_q_norm, l1_k_norm, l1_out_proj, l1_ffn_norm, l1_ffn_ada_w, l1_ffn_ada_b, l1_exp_gate, l1_exp_up, l1_exp_down, router_w, router_b):
    b, s, d = x.shape
    t = b * s
    c8 = jnp.concatenate([c, jnp.zeros((8 - b, d), c.dtype)], axis=0)
    conv_c = l0_conv_a.shape[1]
    q_rank = l1_q_a_norm.shape[0]
    kv_rank = l1_kv_a_norm.shape[0]

    shift, scale, gate = _split_ada(_ada(c8, l0_mix_ada_w, l0_mix_ada_b), b, d)
    u = _norm_mod(x, l0_mix_norm, scale, shift, BF16)
    a_bg, v, ug = _l0_in(u.reshape(t, d), l0_in_proj.astype(BF16), conv_c)
    ycat = _l0_conv(a_bg.reshape(b, s, conv_c), v.reshape(b, s, conv_c), ug.reshape(b, s, conv_c),
                    l0_conv_a, l0_conv_b, l0_conv_b_bias, l0_conv_b_ln_g, l0_conv_b_ln_b)
    h = _out_res(ycat, l0_out_proj.astype(BF16), x, gate)

    shift, scale, gate = _split_ada(_ada(c8, l0_ffn_ada_w, l0_ffn_ada_b), b, d)
    h = _moe(h, l0_ffn_norm, scale, shift, gate, router_w, router_b,
             l0_exp_gate.astype(BF16), l0_exp_up.astype(BF16), l0_exp_down.astype(BF16))

    shift, scale, gate = _split_ada(_ada(c8, l1_mix_ada_w, l1_mix_ada_b), b, d)
    u = _norm_mod(h, l1_mix_norm, scale, shift, BF16)
    tabs = _rope_tables(positions)
    n_in = l1_in_proj.shape[1]
    w_in_pad = jnp.concatenate(
        [l1_in_proj, jnp.zeros((d, q_rank + kv_rank + LANES - n_in), l1_in_proj.dtype)], axis=1).astype(BF16)
    gk_pe = _pad_lanes(l1_k_norm[QK_NOPE:], LANES)
    cq, ckv, kr, kss = _mla_in(u.reshape(t, d), w_in_pad, l1_q_a_norm, l1_kv_a_norm, gk_pe, tabs, q_rank, kv_rank)
    gq_slab = _pad_lanes(l1_q_norm, HEAD_SLAB) * (QK_HEAD ** -0.5 * LOG2_E)
    q = _q_up(cq, _q_slab_weights(l1_q_up), gq_slab, tabs, b, s)
    k, vv = _kv_up(ckv, l1_kv_up.astype(BF16), l1_k_norm[:QK_NOPE].reshape(1, QK_NOPE), kr, kss, b, s)
    o = _flash(q, k, vv)
    h = _out_res(o, l1_out_proj.astype(BF16), h, gate)

    shift, scale, gate = _split_ada(_ada(c8, l1_ffn_ada_w, l1_ffn_ada_b), b, d)
    h = _moe(h, l1_ffn_norm, scale, shift, gate, router_w, router_b,
             l1_exp_gate.astype(BF16), l1_exp_up.astype(BF16), l1_exp_down.astype(BF16))
    return h
```

```python
import functools

import jax
import jax.numpy as jnp
from jax import lax
from jax.experimental import pallas as pl
from jax.experimental.pallas import tpu as pltpu

F32 = jnp.float32
BF16 = jnp.bfloat16
I32 = jnp.int32

EPS = 1e-6
ROPE_THETA = 10000.0

N_EXPERTS = 16
N_GROUPS = 4
GROUP_SIZE = N_EXPERTS // N_GROUPS
CONV_A_WIDTH = 3
CONV_B_WIDTH = 31
CONV_HALO = 32
MLA_HEADS = 32
QK_NOPE = 128
QK_ROPE = 64
QK_HEAD = QK_NOPE + QK_ROPE
V_HEAD = 128
HEAD_SLAB = 256
LANES = 128
FLASH_TILE = 512
FLASH_LAG = 2
FLASH_SLOTS = 2 * FLASH_LAG
LOG2_E = 1.4426950408889634

MOE_TILE = 512
ROW_CHUNK = 32
DMA_UNROLL = 8
VMEM_LIMIT = 56 * 1024 * 1024


def _cparams(n_axes, vmem=VMEM_LIMIT):
    return pltpu.CompilerParams(dimension_semantics=("arbitrary",) * n_axes, vmem_limit_bytes=vmem)


def _dot(a, b):
    return jnp.dot(a, b, preferred_element_type=F32)


def _dot_nt(a, b):
    return lax.dot_general(a, b, (((1,), (1,)), ((), ())), preferred_element_type=F32)


def _sigmoid(x):
    return 1.0 / (1.0 + jnp.exp(-x))


def _ada_kernel(c_ref, w_ref, b_ref, o_ref):
    c = c_ref[...]
    cond = (c * _sigmoid(c)).astype(BF16)
    o_ref[...] = _dot(cond, w_ref[...].astype(BF16)) + b_ref[...]


def _ada(c8, w, b):
    d, n = w.shape
    tn = 512
    return pl.pallas_call(
        _ada_kernel,
        grid=(n // tn,),
        in_specs=[pl.BlockSpec((8, d), lambda j: (0, 0)),
                  pl.BlockSpec((d, tn), lambda j: (0, j)),
                  pl.BlockSpec((1, tn), lambda j: (0, j))],
        out_specs=pl.BlockSpec((8, tn), lambda j: (0, j)),
        out_shape=jax.ShapeDtypeStruct((8, n), F32),
        compiler_params=_cparams(1),
        name="ada",
    )(c8, w, b.reshape(1, n))


def _split_ada(m, batch, d):
    m = m[:batch]
    return (m[:, None, 0:d], m[:, None, d:2 * d], m[:, None, 2 * d:3 * d])


def _modulated_norm(h, g, scale, shift):
    inv = lax.rsqrt(jnp.mean(h * h, axis=-1, keepdims=True) + EPS)
    return (h * inv) * (g * (1.0 + scale)) + shift


def _norm_mod_kernel(h_ref, g_ref, sc_ref, sh_ref, o_ref):
    o_ref[...] = _modulated_norm(h_ref[...], g_ref[...], sc_ref[...], sh_ref[...]).astype(o_ref.dtype)


def _norm_mod(h, g, scale, shift, out_dtype, tm=256):
    b, s, d = h.shape
    return pl.pallas_call(
        _norm_mod_kernel,
        grid=(b, s // tm),
        in_specs=[pl.BlockSpec((None, tm, d), lambda bi, i: (bi, i, 0)),
                  pl.BlockSpec((1, d), lambda bi, i: (0, 0)),
                  pl.BlockSpec((None, 1, d), lambda bi, i: (bi, 0, 0)),
                  pl.BlockSpec((None, 1, d), lambda bi, i: (bi, 0, 0))],
        out_specs=pl.BlockSpec((None, tm, d), lambda bi, i: (bi, i, 0)),
        out_shape=jax.ShapeDtypeStruct((b, s, d), out_dtype),
        compiler_params=_cparams(2),
        name="norm_mod",
    )(h, g.reshape(1, d), scale, shift)


def _l0_in_kernel(u_ref, w_bg, w_cg, w_x, w_val, w_gate, o_bg, o_v, o_ug):
    u = u_ref[...]
    o_bg[...] = _dot(u, w_bg[...]).astype(BF16)
    o_v[...] = (_dot(u, w_cg[...]) * _dot(u, w_x[...])).astype(BF16)
    o_ug[...] = (_dot(u, w_val[...]) * _sigmoid(_dot(u, w_gate[...]))).astype(BF16)


def _l0_in(u, w_in, c, tm=512, tn=256):
    t, d = u.shape
    nj = c // tn
    w_specs = [pl.BlockSpec((d, tn), functools.partial(lambda i, j, g: (0, g * nj + j), g=g)) for g in range(5)]
    out_spec = pl.BlockSpec((tm, tn), lambda i, j: (i, j))
    out_sds = jax.ShapeDtypeStruct((t, c), BF16)
    return pl.pallas_call(
        _l0_in_kernel,
        grid=(t // tm, nj),
        in_specs=[pl.BlockSpec((tm, d), lambda i, j: (i, 0))] + w_specs,
        out_specs=[out_spec] * 3,
        out_shape=[out_sds] * 3,
        compiler_params=_cparams(2),
        name="l0_in",
    )(u, w_in, w_in, w_in, w_in, w_in)


def _l0_conv_kernel(bg_ref, v_ref, ug_ref, vh_ref, ugh_ref, wa_ref, wb_ref, bb_ref, lg_ref, lb_ref,
                    o_ref, vs_ref, us_ref, ub_ref, *, ts, c, rc):
    first = pl.program_id(1) == 0
    keep = jnp.where(first, 0.0, 1.0)
    vs_ref[0:CONV_HALO, :] = vh_ref[...].astype(F32) * keep
    us_ref[0:CONV_HALO, :] = ugh_ref[...].astype(F32) * keep
    vs_ref[CONV_HALO:, :] = v_ref[...].astype(F32)
    us_ref[CONV_HALO:, :] = ug_ref[...].astype(F32)

    win = rc + CONV_HALO

    def causal_conv(w_ref, width, window, cols, acc):
        for res in range(8):
            taps = [k for k in range(width) if (CONV_HALO - (width - 1) + k) % 8 == res]
            if not taps:
                continue
            shifted = window if res == 0 else pltpu.roll(window, win - res, 0)
            for k in taps:
                a0 = (CONV_HALO - (width - 1) + k) - res
                acc = acc + w_ref[k:k + 1, cols] * shifted[a0:a0 + rc, :]
        return acc

    def rows(r, carry):
        r0 = pl.multiple_of(r * rc, rc)
        for cj in range(c // LANES):
            cols = pl.ds(cj * LANES, LANES)
            acc = causal_conv(wa_ref, CONV_A_WIDTH, vs_ref[pl.ds(r0, win), cols], cols, jnp.zeros((rc, LANES), F32))
            o_ref[pl.ds(r0, rc), cols] = (bg_ref[pl.ds(r0, rc), cols].astype(F32) * acc).astype(BF16)
            acc = causal_conv(wb_ref, CONV_B_WIDTH, us_ref[pl.ds(r0, win), cols], cols,
                              jnp.zeros((rc, LANES), F32) + bb_ref[:, cols])
            ub_ref[pl.ds(r0, rc), cols] = acc
        return carry

    lax.fori_loop(0, ts // rc, rows, 0)

    ub = ub_ref[...]
    mu = jnp.mean(ub, axis=-1, keepdims=True)
    xc = ub - mu
    var = jnp.mean(xc * xc, axis=-1, keepdims=True)
    y = xc * lax.rsqrt(var + EPS) * lg_ref[...] + lb_ref[...]
    o_ref[:, c:] = (y * _sigmoid(y)).astype(BF16)


def _l0_conv(a_bg, v, ug, conv_a, conv_b, conv_b_bias, ln_g, ln_b, ts=256, rc=32):
    b, s, c = v.shape
    hb = ts // CONV_HALO
    main = pl.BlockSpec((None, ts, c), lambda bi, i: (bi, i, 0))
    halo = pl.BlockSpec((None, CONV_HALO, c), lambda bi, i: (bi, jnp.maximum(i * hb - 1, 0), 0))
    wb_rows = 32
    conv_b_pad = jnp.zeros((wb_rows, c), F32).at[:CONV_B_WIDTH].set(conv_b)
    conv_a_pad = jnp.zeros((8, c), F32).at[:CONV_A_WIDTH].set(conv_a)
    full = lambda r: pl.BlockSpec((r, c), lambda bi, i: (0, 0))
    return pl.pallas_call(
        functools.partial(_l0_conv_kernel, ts=ts, c=c, rc=rc),
        grid=(b, s // ts),
        in_specs=[main, main, main, halo, halo, full(8), full(wb_rows), full(1), full(1), full(1)],
        out_specs=pl.BlockSpec((None, ts, 2 * c), lambda bi, i: (bi, i, 0)),
        out_shape=jax.ShapeDtypeStruct((b, s, 2 * c), BF16),
        scratch_shapes=[pltpu.VMEM((CONV_HALO + ts, c), F32),
                        pltpu.VMEM((CONV_HALO + ts, c), F32),
                        pltpu.VMEM((ts, c), F32)],
        compiler_params=_cparams(2),
        name="l0_conv",
    )(a_bg, v, ug, v, ug, conv_a_pad, conv_b_pad, conv_b_bias.reshape(1, c), ln_g.reshape(1, c), ln_b.reshape(1, c))


def _out_res_kernel(a_ref, w_ref, h_ref, gate_ref, o_ref):
    o_ref[...] = h_ref[...] + gate_ref[...] * _dot(a_ref[...], w_ref[...])


def _out_res(a, w, h, gate, tm=1024, tn=512):
    b, s, k = a.shape
    d = w.shape[1]
    return pl.pallas_call(
        _out_res_kernel,
        grid=(b, s // tm, d // tn),
        in_specs=[pl.BlockSpec((None, tm, k), lambda bi, i, j: (bi, i, 0)),
                  pl.BlockSpec((k, tn), lambda bi, i, j: (0, j)),
                  pl.BlockSpec((None, tm, tn), lambda bi, i, j: (bi, i, j)),
                  pl.BlockSpec((None, 1, tn), lambda bi, i, j: (bi, 0, j))],
        out_specs=pl.BlockSpec((None, tm, tn), lambda bi, i, j: (bi, i, j)),
        out_shape=jax.ShapeDtypeStruct((b, s, d), F32),
        compiler_params=_cparams(3),
        name="out_res",
    )(a, w, h, gate)


def _route_kernel(h_ref, g_ref, sc_ref, sh_ref, rw_ref, rb_ref, ri_ref, rwt_ref, cnt_ref,
                  tri_ref, carry_ref, *, tm):
    step0 = jnp.logical_and(pl.program_id(0) == 0, pl.program_id(1) == 0)

    @pl.when(step0)
    def _():
        carry_ref[...] = jnp.zeros_like(carry_ref)
        src = lax.broadcasted_iota(I32, (tm, tm), 0)
        dst = lax.broadcasted_iota(I32, (tm, tm), 1)
        tri_ref[...] = jnp.where(src < dst, 1.0, 0.0).astype(BF16)

    u = _modulated_norm(h_ref[...], g_ref[...], sc_ref[...], sh_ref[...])
    u_hi = u.astype(BF16)
    u_lo = (u - u_hi.astype(F32)).astype(BF16)
    rw = rw_ref[...]
    rw_hi = rw.astype(BF16)
    rw_lo = (rw - rw_hi.astype(F32)).astype(BF16)
    logits = _dot_nt(rw_hi, u_hi) + (_dot_nt(rw_hi, u_lo) + _dot_nt(rw_lo, u_hi))
    aff = _sigmoid(logits)
    sel = aff + rb_ref[...]
    row = lax.broadcasted_iota(I32, (N_EXPERTS, tm), 0)

    best, best_g = None, None
    for g in range(N_GROUPS):
        s = [sel[g * GROUP_SIZE + j:g * GROUP_SIZE + j + 1, :] for j in range(GROUP_SIZE)]
        score = None
        for a in range(GROUP_SIZE):
            for bb in range(a + 1, GROUP_SIZE):
                pair = s[a] + s[bb]
                score = pair if score is None else jnp.maximum(score, pair)
        if best is None:
            best, best_g = score, jnp.zeros((1, tm), I32)
        else:
            upd = score > best
            best = jnp.where(upd, score, best)
            best_g = jnp.where(upd, g, best_g)

    neg = -jnp.inf
    masked = jnp.where((row // GROUP_SIZE) == best_g, sel, neg)
    m1 = jnp.max(masked, axis=0, keepdims=True)
    i1 = jnp.min(jnp.where(masked == m1, row, N_EXPERTS), axis=0, keepdims=True)
    masked2 = jnp.where(row == i1, neg, masked)
    m2 = jnp.max(masked2, axis=0, keepdims=True)
    i2 = jnp.min(jnp.where(masked2 == m2, row, N_EXPERTS), axis=0, keepdims=True)
    hit1 = row == i1
    hit2 = row == i2
    a1 = jnp.sum(jnp.where(hit1, aff, 0.0), axis=0, keepdims=True)
    a2 = jnp.sum(jnp.where(hit2, aff, 0.0), axis=0, keepdims=True)
    den = a1 + a2
    w1 = a1 / den
    w2 = a2 / den

    onehot = jnp.where(jnp.logical_or(hit1, hit2), 1.0, 0.0)
    before = _dot(onehot.astype(BF16), tri_ref[...]) + carry_ref[...]
    r1 = jnp.sum(jnp.where(hit1, before, 0.0), axis=0, keepdims=True).astype(I32)
    r2 = jnp.sum(jnp.where(hit2, before, 0.0), axis=0, keepdims=True).astype(I32)
    carry_ref[...] = carry_ref[...] + jnp.sum(onehot, axis=1, keepdims=True)

    zi = jnp.zeros((4, tm), I32)
    ri_ref[...] = jnp.concatenate([i1, i2, r1, r2, zi], axis=0)
    rwt_ref[...] = jnp.concatenate([w1, w2, jnp.zeros((6, tm), F32)], axis=0)
    cnt_ref[...] = jnp.broadcast_to(carry_ref[...], cnt_ref.shape)


def _route(h, g, scale, shift, router_w, router_b, tm=512):
    b, s, d = h.shape
    t = b * s
    n_i = s // tm
    return pl.pallas_call(
        functools.partial(_route_kernel, tm=tm),
        grid=(b, n_i),
        in_specs=[pl.BlockSpec((None, tm, d), lambda bi, i: (bi, i, 0)),
                  pl.BlockSpec((1, d), lambda bi, i: (0, 0)),
                  pl.BlockSpec((None, 1, d), lambda bi, i: (bi, 0, 0)),
                  pl.BlockSpec((None, 1, d), lambda bi, i: (bi, 0, 0)),
                  pl.BlockSpec((N_EXPERTS, d), lambda bi, i: (0, 0)),
                  pl.BlockSpec((N_EXPERTS, 1), lambda bi, i: (0, 0))],
        out_specs=[pl.BlockSpec((8, tm), lambda bi, i: (0, bi * n_i + i)),
                   pl.BlockSpec((8, tm), lambda bi, i: (0, bi * n_i + i)),
                   pl.BlockSpec((N_EXPERTS, LANES), lambda bi, i: (0, 0))],
        out_shape=[jax.ShapeDtypeStruct((8, t), I32),
                   jax.ShapeDtypeStruct((8, t), F32),
                   jax.ShapeDtypeStruct((N_EXPERTS, LANES), F32)],
        scratch_shapes=[pltpu.VMEM((tm, tm), BF16), pltpu.VMEM((N_EXPERTS, 1), F32)],
        compiler_params=_cparams(2),
        name="route",
    )(h, g.reshape(1, d), scale, shift, router_w.T, router_b.reshape(N_EXPERTS, 1))


def _pack_bf16_pairs(u):
    half = u.shape[1] // 2
    lo = lax.bitcast_convert_type(u[:, :half].astype(BF16).astype(F32), jnp.uint32)
    hi = lax.bitcast_convert_type(u[:, half:].astype(BF16).astype(F32), jnp.uint32)
    return (lo >> 16) | (hi & jnp.uint32(0xFFFF0000))


def _unpack_bf16_pairs(w):
    lo = lax.bitcast_convert_type(w << 16, F32).astype(BF16)
    hi = lax.bitcast_convert_type(w & jnp.uint32(0xFFFF0000), F32).astype(BF16)
    return jnp.concatenate([lo, hi], axis=1)


def _dispatch_kernel(off_ref, cnt_ref, nt_ref, ri_ref, h_ref, g_ref, sc_ref, sh_ref, xs_hbm,
                     xp_ref, zero_ref, sem, *, tb):
    step0 = jnp.logical_and(pl.program_id(0) == 0, pl.program_id(1) == 0)
    zrows = zero_ref.shape[0]

    def row_copy(tt, pos, k):
        return pltpu.make_async_copy(xp_ref.at[pl.ds(tt, 1)], xs_hbm.at[pl.ds(pos, 1)], sem.at[k])

    def pad_copy(pos):
        return pltpu.make_async_copy(zero_ref.at[pl.ds(0, 1)], xs_hbm.at[pl.ds(pos, 1)], sem.at[2])

    def tail_copy(blk):
        start = pl.multiple_of(nt_ref[0] * MOE_TILE + blk * zrows, zrows)
        return pltpu.make_async_copy(zero_ref, xs_hbm.at[pl.ds(start, zrows)], sem.at[2])

    @pl.when(step0)
    def _():
        zero_ref[...] = jnp.zeros_like(zero_ref)
        n_tail =(xs_hbm.shape[0] // MOE_TILE - nt_ref[0]) * (MOE_TILE // zrows)
        lax.fori_loop(0, n_tail, lambda blk, c: (tail_copy(blk).start(), c)[1], 0)
        lax.fori_loop(0, n_tail, lambda blk, c: (tail_copy(blk).wait(), c)[1], 0)
        for e in range(N_EXPERTS):
            n_pad = (MOE_TILE - (cnt_ref[e] & (MOE_TILE - 1))) & (MOE_TILE - 1)
            base = off_ref[e] + cnt_ref[e]

            def start(r, c, base=base):
                pad_copy(base + r).start()
                return c

            def wait(r, c, base=base):
                pad_copy(base + r).wait()
                return c

            lax.fori_loop(0, n_pad, start, 0)
            lax.fori_loop(0, n_pad, wait, 0)

    def chunk(r, c):
        rs = pl.ds(pl.multiple_of(r * ROW_CHUNK, ROW_CHUNK), ROW_CHUNK)
        xp_ref[rs, :] = _pack_bf16_pairs(_modulated_norm(h_ref[rs, :], g_ref[...], sc_ref[...], sh_ref[...]))
        return c

    lax.fori_loop(0, tb // ROW_CHUNK, chunk, 0)

    def start(tt, c):
        for k in range(2):
            pos = off_ref[ri_ref[k, tt]] + ri_ref[2 + k, tt]
            row_copy(tt, pos, k).start()
        return c

    def wait(tt, c):
        for k in range(2):
            row_copy(0, 0, k).wait()
        return c

    lax.fori_loop(0, tb, start, 0, unroll=DMA_UNROLL)
    lax.fori_loop(0, tb, wait, 0, unroll=DMA_UNROLL)


def _dispatch(h, g, scale, shift, ri, off, counts, n_tiles, n_rows, tb=256):
    b, s, d = h.shape
    n_i = s // tb
    grid_spec = pltpu.PrefetchScalarGridSpec(
        num_scalar_prefetch=3,
        grid=(b, n_i),
        in_specs=[pl.BlockSpec((8, tb), lambda bi, i, off, cnt, nt: (0, bi * n_i + i), memory_space=pltpu.SMEM),
                  pl.BlockSpec((None, tb, d), lambda bi, i, off, cnt, nt: (bi, i, 0)),
                  pl.BlockSpec((1, d), lambda bi, i, off, cnt, nt: (0, 0)),
                  pl.BlockSpec((None, 1, d), lambda bi, i, off, cnt, nt: (bi, 0, 0)),
                  pl.BlockSpec((None, 1, d), lambda bi, i, off, cnt, nt: (bi, 0, 0))],
        out_specs=pl.BlockSpec(memory_space=pl.ANY),
        scratch_shapes=[pltpu.VMEM((tb, d // 2), jnp.uint32), pltpu.VMEM((64, d // 2), jnp.uint32),
                        pltpu.SemaphoreType.DMA((3,))],
    )
    return pl.pallas_call(
        functools.partial(_dispatch_kernel, tb=tb),
        grid_spec=grid_spec,
        out_shape=jax.ShapeDtypeStruct((n_rows, d // 2), jnp.uint32),
        compiler_params=_cparams(2),
        name="dispatch",
    )(off, counts, n_tiles, ri, h, g.reshape(1, d), scale, shift)


def _experts_kernel(te_ref, nt_ref, xs_ref, wg_ref, wu_ref, wd_ref, ys_ref, xb_ref):
    j = pl.program_id(0)
    cidx = pl.program_id(1)

    valid = j < nt_ref[0]

    def hidden_chunk(first):
        if first:
            xb_ref[...] = _unpack_bf16_pairs(xs_ref[...])
        x = xb_ref[...]
        gate = _dot(x, wg_ref[...])
        up = _dot(x, wu_ref[...])
        he = (gate * _sigmoid(gate) * up).astype(BF16)
        d = ys_ref.shape[1]
        dn = min(d, 1024)
        for dj in range(d // dn):
            cols = pl.ds(dj * dn, dn)
            y = _dot(he, wd_ref[:, cols])
            ys_ref[:, cols] = y if first else ys_ref[:, cols] + y

    pl.when(jnp.logical_and(valid, cidx == 0))(lambda: hidden_chunk(True))
    pl.when(jnp.logical_and(valid, cidx != 0))(lambda: hidden_chunk(False))

    @pl.when(jnp.logical_and(jnp.logical_not(valid), cidx == 0))
    def _():
        ys_ref[...] = jnp.zeros_like(ys_ref)


def _experts(xs, w_gate, w_up, w_down, tile_expert, n_tiles, tf=256):
    n_rows = xs.shape[0]
    d, f = w_gate.shape[1:]
    nt_max = n_rows // MOE_TILE
    nc = f // tf

    def tile(j, c, te, nt):
        return jnp.minimum(j, nt[0] - 1)

    def chunk(j, c, te, nt):
        return jnp.where(j < nt[0], c, nc - 1)

    grid_spec = pltpu.PrefetchScalarGridSpec(
        num_scalar_prefetch=2,
        grid=(nt_max, nc),
        in_specs=[pl.BlockSpec((MOE_TILE, d // 2), lambda j, c, te, nt: (tile(j, c, te, nt), 0)),
                  pl.BlockSpec((None, d, tf), lambda j, c, te, nt: (te[tile(j, c, te, nt)], 0, chunk(j, c, te, nt))),
                  pl.BlockSpec((None, d, tf), lambda j, c, te, nt: (te[tile(j, c, te, nt)], 0, chunk(j, c, te, nt))),
                  pl.BlockSpec((None, tf, d), lambda j, c, te, nt: (te[tile(j, c, te, nt)], chunk(j, c, te, nt), 0))],
        out_specs=pl.BlockSpec((MOE_TILE, d), lambda j, c, te, nt: (j, 0)),
        scratch_shapes=[pltpu.VMEM((MOE_TILE, d), BF16)],
    )
    return pl.pallas_call(
        _experts_kernel,
        grid_spec=grid_spec,
        out_shape=jax.ShapeDtypeStruct((n_rows, d), F32),
        compiler_params=_cparams(2),
        name="experts",
    )(tile_expert, n_tiles, xs, w_gate, w_up, w_down)


def _combine_kernel(off_ref, ri_ref, w_ref, h_ref, gate_ref, ys_hbm, o_ref, y0_ref, y1_ref, sem, *, tc):
    bufs = (y0_ref, y1_ref)

    def row_copy(pos, tt, k):
        return pltpu.make_async_copy(ys_hbm.at[pl.ds(pos, 1)], bufs[k].at[pl.ds(tt, 1)], sem.at[k])

    def start(tt, c):
        for k in range(2):
            pos = off_ref[ri_ref[k, tt]] + ri_ref[2 + k, tt]
            row_copy(pos, tt, k).start()
        return c

    def wait(tt, c):
        for k in range(2):
            row_copy(0, 0, k).wait()
        return c

    lax.fori_loop(0, tc, start, 0, unroll=DMA_UNROLL)
    lax.fori_loop(0, tc, wait, 0, unroll=DMA_UNROLL)

    def chunk(r, c):
        rs = pl.ds(pl.multiple_of(r * ROW_CHUNK, ROW_CHUNK), ROW_CHUNK)
        w = w_ref[rs, :]
        y = w[:, 0:1] * y0_ref[rs, :] + w[:, 1:2] * y1_ref[rs, :]
        o_ref[rs, :] = h_ref[rs, :] + gate_ref[...] * y
        return c

    lax.fori_loop(0, tc // ROW_CHUNK, chunk, 0)


def _combine(h, gate, ys, ri, wts, off, tc=256):
    b, s, d = h.shape
    n_i = s // tc
    grid_spec = pltpu.PrefetchScalarGridSpec(
        num_scalar_prefetch=1,
        grid=(b, n_i),
        in_specs=[pl.BlockSpec((8, tc), lambda bi, i, off: (0, bi * n_i + i), memory_space=pltpu.SMEM),
                  pl.BlockSpec((tc, 2), lambda bi, i, off: (bi * n_i + i, 0)),
                  pl.BlockSpec((None, tc, d), lambda bi, i, off: (bi, i, 0)),
                  pl.BlockSpec((None, 1, d), lambda bi, i, off: (bi, 0, 0)),
                  pl.BlockSpec(memory_space=pl.ANY)],
        out_specs=pl.BlockSpec((None, tc, d), lambda bi, i, off: (bi, i, 0)),
        scratch_shapes=[pltpu.VMEM((tc, d), F32), pltpu.VMEM((tc, d), F32), pltpu.SemaphoreType.DMA((2,))],
    )
    return pl.pallas_call(
        functools.partial(_combine_kernel, tc=tc),
        grid_spec=grid_spec,
        out_shape=jax.ShapeDtypeStruct((b, s, d), F32),
        compiler_params=_cparams(2),
        name="combine",
    )(off, ri, wts, h, gate, ys)


def _moe(h, g, scale, shift, gate, router_w, router_b, w_gate, w_up, w_down):
    b, s, d = h.shape
    t = b * s
    nt_max = (2 * t) // MOE_TILE + N_EXPERTS
    ri, rwt, cnt = _route(h, g, scale, shift, router_w, router_b)
    counts = cnt[:, 0].astype(I32)
    tiles_per = (counts + MOE_TILE - 1) // MOE_TILE
    tile_end = jnp.cumsum(tiles_per)
    off = (tile_end - tiles_per) * MOE_TILE
    tile_expert = jnp.minimum(
        jnp.searchsorted(tile_end, jnp.arange(nt_max, dtype=I32), side="right"), N_EXPERTS - 1).astype(I32)
    n_tiles = tile_end[-1:].astype(I32)
    xs = _dispatch(h, g, scale, shift, ri, off, counts, n_tiles, nt_max * MOE_TILE)
    ys = _experts(xs, w_gate, w_up, w_down, tile_expert, n_tiles)
    return _combine(h, gate, ys, ri, rwt[0:2].T, off)


def _rope_tables_kernel(pos_ref, invf_ref, c_ref, s1_ref, s2_ref):
    ang = pos_ref[...].astype(F32) * invf_ref[...]
    lane = lax.broadcasted_iota(I32, ang.shape, 1)
    cos = jnp.cos(ang)
    sin = jnp.sin(ang)
    half = QK_ROPE // 2
    c_ref[...] = jnp.where(lane < QK_ROPE, cos, 0.0)
    s1_ref[...] = jnp.where(lane < half, -sin, 0.0)
    s2_ref[...] = jnp.where(jnp.logical_and(lane >= half, lane < QK_ROPE), sin, 0.0)


def _rope_tables(positions, tm=1024):
    t = positions.size
    half = QK_ROPE // 2
    inv_freq = 1.0 / (ROPE_THETA ** (jnp.arange(0, QK_ROPE, 2, dtype=F32) / QK_ROPE))
    invf = jnp.concatenate([inv_freq, inv_freq, jnp.zeros((LANES - 2 * half,), F32)]).reshape(1, LANES)
    spec = pl.BlockSpec((tm, LANES), lambda i: (i, 0))
    sds = jax.ShapeDtypeStruct((t, LANES), F32)
    return pl.pallas_call(
        _rope_tables_kernel,
        grid=(t // tm,),
        in_specs=[pl.BlockSpec((tm, 1), lambda i: (i, 0)), pl.BlockSpec((1, LANES), lambda i: (0, 0))],
        out_specs=[spec] * 3,
        out_shape=[sds] * 3,
        compiler_params=_cparams(1),
        name="rope_tables",
    )(positions.reshape(t, 1), invf)


def _rope(p, c, s1, s2):
    return p * c + pltpu.roll(p, 96, 1) * s1 + pltpu.roll(p, 32, 1) * s2


def _mla_in_kernel(u_ref, w_ref, gq_ref, gkv_ref, gkpe_ref, c_ref, s1_ref, s2_ref,
                   cq_ref, ckv_ref, kr_ref, kss_ref, *, q_rank, kv_rank):
    proj = _dot(u_ref[...], w_ref[...])
    cq = proj[:, :q_rank]
    cq_ref[...] = (cq * lax.rsqrt(jnp.mean(cq * cq, axis=-1, keepdims=True) + EPS) * gq_ref[...]).astype(BF16)
    ckv = proj[:, q_rank:q_rank + kv_rank]
    ckv_ref[...] = (ckv * lax.rsqrt(jnp.mean(ckv * ckv, axis=-1, keepdims=True) + EPS) * gkv_ref[...]).astype(BF16)
    kpe = proj[:, q_rank + kv_rank:]
    kss_ref[...] = jnp.broadcast_to(jnp.sum(kpe * kpe, axis=-1, keepdims=True), kss_ref.shape)
    kr_ref[...] = _rope(kpe * gkpe_ref[...], c_ref[...], s1_ref[...], s2_ref[...])


def _mla_in(u, w_in_pad, gq, gkv, gkpe, tabs, q_rank, kv_rank, tm=512):
    t, d = u.shape
    n = w_in_pad.shape[1]
    row = lambda w: pl.BlockSpec((tm, w), lambda i: (i, 0))
    const = lambda w: pl.BlockSpec((1, w), lambda i: (0, 0))
    return pl.pallas_call(
        functools.partial(_mla_in_kernel, q_rank=q_rank, kv_rank=kv_rank),
        grid=(t // tm,),
        in_specs=[row(d), pl.BlockSpec((d, n), lambda i: (0, 0)), const(q_rank), const(kv_rank), const(LANES),
                  row(LANES), row(LANES), row(LANES)],
        out_specs=[row(q_rank), row(kv_rank), row(LANES), row(LANES)],
        out_shape=[jax.ShapeDtypeStruct((t, q_rank), BF16), jax.ShapeDtypeStruct((t, kv_rank), BF16),
                   jax.ShapeDtypeStruct((t, LANES), F32), jax.ShapeDtypeStruct((t, LANES), F32)],
        compiler_params=_cparams(1),
        name="mla_in",
    )(u, w_in_pad, gq.reshape(1, -1), gkv.reshape(1, -1), gkpe, *tabs)


def _q_up_kernel(cq_ref, w_ref, g_ref, c_ref, s1_ref, s2_ref, q_ref, *, hb):
    cq = cq_ref[...]
    g = g_ref[...]
    for hh in range(hb):
        slab = _dot(cq, w_ref[:, hh * HEAD_SLAB:(hh + 1) * HEAD_SLAB])
        inv = lax.rsqrt(jnp.sum(slab * slab, axis=-1, keepdims=True) * (1.0 / QK_HEAD) + EPS)
        y = slab * inv * g
        rot = _rope(y[:, QK_NOPE:], c_ref[...], s1_ref[...], s2_ref[...])
        q_ref[hh] = jnp.concatenate([y[:, :QK_NOPE], rot], axis=1).astype(BF16)


def _q_up(cq, w_q_slab, g_slab, tabs, b, s, tm=1024, hb=8):
    t, r = cq.shape
    n_i = s // tm
    row = lambda w: pl.BlockSpec((tm, w), lambda bi, i, hj: (bi * n_i + i, 0))
    return pl.pallas_call(
        functools.partial(_q_up_kernel, hb=hb),
        grid=(b, n_i, MLA_HEADS // hb),
        in_specs=[row(r), pl.BlockSpec((r, hb * HEAD_SLAB), lambda bi, i, hj: (0, hj)),
                  pl.BlockSpec((1, HEAD_SLAB), lambda bi, i, hj: (0, 0)), row(LANES), row(LANES), row(LANES)],
        out_specs=pl.BlockSpec((None, hb, tm, HEAD_SLAB), lambda bi, i, hj: (bi, hj, i, 0)),
        out_shape=jax.ShapeDtypeStruct((b, MLA_HEADS, s, HEAD_SLAB), BF16),
        compiler_params=_cparams(3),
        name="q_up",
    )(cq, w_q_slab, g_slab, *tabs)


def _kv_up_kernel(ckv_ref, w_ref, g_ref, kr_ref, kss_ref, k_ref, vt_ref, *, hb):
    ckv = ckv_ref[...]
    g = g_ref[...]
    kr = kr_ref[...]
    kss = kss_ref[:, 0:1]
    n_blk, _, tk = vt_ref.shape[1:]
    for hh in range(hb):
        slab = _dot(ckv, w_ref[:, hh * HEAD_SLAB:(hh + 1) * HEAD_SLAB])
        kn = slab[:, :QK_NOPE]
        inv = lax.rsqrt((jnp.sum(kn * kn, axis=-1, keepdims=True) + kss) * (1.0 / QK_HEAD) + EPS)
        k_ref[hh] = jnp.concatenate([kn * inv * g, kr * inv], axis=1).astype(BF16)
        vt = slab[:, QK_NOPE:].T
        for blk in range(n_blk):
            vt_ref[hh, blk] = vt[:, blk * tk:(blk + 1) * tk].astype(BF16)


def _kv_up(ckv, w_kv, g_nope, kr, kss, b, s, tm=1024, hb=8):
    t, r = ckv.shape
    n_i = s // tm
    n_blk = tm // FLASH_TILE
    row = lambda w: pl.BlockSpec((tm, w), lambda bi, i, hj: (bi * n_i + i, 0))
    return pl.pallas_call(
        functools.partial(_kv_up_kernel, hb=hb),
        grid=(b, n_i, MLA_HEADS // hb),
        in_specs=[row(r), pl.BlockSpec((r, hb * HEAD_SLAB), lambda bi, i, hj: (0, hj)),
                  pl.BlockSpec((1, QK_NOPE), lambda bi, i, hj: (0, 0)), row(LANES), row(LANES)],
        out_specs=[pl.BlockSpec((None, hb, tm, HEAD_SLAB), lambda bi, i, hj: (bi, hj, i, 0)),
                   pl.BlockSpec((None, hb, n_blk, V_HEAD, FLASH_TILE), lambda bi, i, hj: (bi, hj, i, 0, 0))],
        out_shape=[jax.ShapeDtypeStruct((b, MLA_HEADS, s, HEAD_SLAB), BF16),
                   jax.ShapeDtypeStruct((b, MLA_HEADS, s // FLASH_TILE, V_HEAD, FLASH_TILE), BF16)],
        compiler_params=_cparams(3),
        name="kv_up",
    )(ckv, w_kv, g_nope, kr, kss)


def _flash_kernel(q_ref, k_ref, vt_ref, o_ref, m_ref, l_ref, acc_ref, s_buf, p_buf, a_buf, *, nq):
    tq = FLASH_TILE
    m_ref[...] = jnp.full_like(m_ref, -jnp.inf)
    l_ref[...] = jnp.zeros_like(l_ref)
    acc_ref[...] = jnp.zeros_like(acc_ref)

    def rows(j):
        return pl.ds(pl.multiple_of(j * tq, tq), tq)

    def stage_a(idx, slot):
        qi, kj = idx
        s_buf[slot] = _dot_nt(k_ref[rows(kj), :], q_ref[rows(qi), :])

    def stage_b(idx, slot, diagonal):
        qi, _ = idx
        st = s_buf[slot]
        if diagonal:
            key = lax.broadcasted_iota(I32, st.shape, 0)
            qry = lax.broadcasted_iota(I32, st.shape, 1)
            st = jnp.where(key <= qry, st, -jnp.inf)
        m_prev = m_ref[qi]
        m_new = jnp.maximum(m_prev, jnp.max(st, axis=0, keepdims=True))
        p = jnp.exp2(st - m_new)
        alpha = jnp.exp2(m_prev - m_new)
        l_ref[qi] = alpha * l_ref[qi] + jnp.sum(p, axis=0, keepdims=True)
        m_ref[qi] = m_new
        a_buf[slot] = alpha
        p_buf[slot] = p.astype(BF16)

    def stage_c(idx, slot):
        qi, kj = idx
        acc_ref[qi] = a_buf[slot] * acc_ref[qi] + _dot(vt_ref[kj], p_buf[slot])

    def pipeline(first, advance, n, diagonal):
        depth = 2 * FLASH_LAG

        def step(win, t_mod, do_a, do_b, do_c):
            if do_a:
                stage_a(win[0], t_mod % FLASH_SLOTS)
            if do_b:
                stage_b(win[FLASH_LAG], (t_mod - FLASH_LAG) % FLASH_SLOTS, diagonal)
            if do_c:
                stage_c(win[depth], (t_mod - depth) % FLASH_SLOTS)
            return (advance(win[0]),) + win[:-1]

        def flatten(win):
            return tuple(x for pair in win for x in pair)

        def unflatten(flat):
            return tuple((flat[2 * i], flat[2 * i + 1]) for i in range(depth + 1))

        win = (first,) + ((jnp.int32(0), jnp.int32(0)),) * depth
        for t in range(depth):
            win = step(win, t, True, t >= FLASH_LAG, False)

        def body(_, flat):
            win = unflatten(flat)
            for u in range(FLASH_SLOTS):
                win = step(win, depth + u, True, True, True)
            return flatten(win)

        win = unflatten(lax.fori_loop(0, (n - depth) // FLASH_SLOTS, body, flatten(win)))
        for t in range(n, n + depth):
            win = step(win, t, False, t - FLASH_LAG < n, True)

    def next_below(idx):
        qi, kj = idx
        wrap = kj + 1 == qi
        return jnp.where(wrap, qi + 1, qi), jnp.where(wrap, 0, kj + 1)

    pipeline((jnp.int32(1), jnp.int32(0)), next_below, nq * (nq - 1) // 2, False)
    pipeline((jnp.int32(0), jnp.int32(0)), lambda idx: (idx[0] + 1, idx[1] + 1), nq, True)

    def finish(qi, c):
        o_ref[rows(qi), :] = (acc_ref[qi] / l_ref[qi]).T.astype(o_ref.dtype)
        return c

    lax.fori_loop(0, nq, finish, 0)


def _flash(q, k, vt):
    b, h, s, _ = q.shape
    tq = FLASH_TILE
    nq = s // tq
    depth = 2 * FLASH_LAG
    for n_blocks in (nq * (nq - 1) // 2, nq):
        assert n_blocks >= depth and (n_blocks - depth) % FLASH_SLOTS == 0, (nq, n_blocks)
    return pl.pallas_call(
        functools.partial(_flash_kernel, nq=nq),
        grid=(b, h),
        in_specs=[pl.BlockSpec((None, None, s, HEAD_SLAB), lambda bi, hi: (bi, hi, 0, 0)),
                  pl.BlockSpec((None, None, s, HEAD_SLAB), lambda bi, hi: (bi, hi, 0, 0)),
                  pl.BlockSpec((None, None, nq, V_HEAD, tq), lambda bi, hi: (bi, hi, 0, 0, 0))],
        out_specs=pl.BlockSpec((None, s, V_HEAD), lambda bi, hi: (bi, 0, hi)),
        out_shape=jax.ShapeDtypeStruct((b, s, h * V_HEAD), BF16),
        scratch_shapes=[pltpu.VMEM((nq, 1, tq), F32), pltpu.VMEM((nq, 1, tq), F32), pltpu.VMEM((nq, V_HEAD, tq), F32),
                        pltpu.VMEM((FLASH_SLOTS, tq, tq), F32), pltpu.VMEM((FLASH_SLOTS, tq, tq), BF16),
                        pltpu.VMEM((FLASH_SLOTS, 1, tq), F32)],
        compiler_params=_cparams(2),
        name="flash",
    )(q, k, vt)


def _q_slab_weights(w_q_up):
    r = w_q_up.shape[0]
    w = w_q_up.reshape(r, MLA_HEADS, QK_HEAD)
    w = jnp.concatenate([w, jnp.zeros((r, MLA_HEADS, HEAD_SLAB - QK_HEAD), w.dtype)], axis=2)
    return w.reshape(r, MLA_HEADS * HEAD_SLAB).astype(BF16)


def _pad_lanes(x, width):
    return jnp.concatenate([x, jnp.zeros((width - x.shape[0],), x.dtype)]).reshape(1, width)


def kernel(x, c, positions, l0_mix_norm, l0_mix_ada_w, l0_mix_ada_b, l0_in_proj, l0_conv_a, l0_conv_b, l0_conv_b_bias, l0_conv_b_ln_g, l0_conv_b_ln_b, l0_out_proj, l0_ffn_norm, l0_ffn_ada_w, l0_ffn_ada_b, l0_exp_gate, l0_exp_up, l0_exp_down, l1_mix_norm, l1_mix_ada_w, l1_mix_ada_b, l1_in_proj, l1_q_a_norm, l1_kv_a_norm, l1_q_up, l1_kv_up, l1_q_norm, l1_k_norm, l1_out_proj, l1_ffn_norm, l1_ffn_ada_w, l1_ffn_ada_b, l1_exp_gate, l1_exp_up, l1_exp_down, router_w, router_b):
    b, s, d = x.shape
    t = b * s
    c8 = jnp.concatenate([c, jnp.zeros((8 - b, d), c.dtype)], axis=0)
    conv_c = l0_conv_a.shape[1]
    q_rank = l1_q_a_norm.shape[0]
    kv_rank = l1_kv_a_norm.shape[0]

    shift, scale, gate = _split_ada(_ada(c8, l0_mix_ada_w, l0_mix_ada_b), b, d)
    u = _norm_mod(x, l0_mix_norm, scale, shift, BF16)
    a_bg, v, ug = _l0_in(u.reshape(t, d), l0_in_proj.astype(BF16), conv_c)
    ycat = _l0_conv(a_bg.reshape(b, s, conv_c), v.reshape(b, s, conv_c), ug.reshape(b, s, conv_c),
                    l0_conv_a, l0_conv_b, l0_conv_b_bias, l0_conv_b_ln_g, l0_conv_b_ln_b)
    h = _out_res(ycat, l0_out_proj.astype(BF16), x, gate)

    shift, scale, gate = _split_ada(_ada(c8, l0_ffn_ada_w, l0_ffn_ada_b), b, d)
    h = _moe(h, l0_ffn_norm, scale, shift, gate, router_w, router_b,
             l0_exp_gate.astype(BF16), l0_exp_up.astype(BF16), l0_exp_down.astype(BF16))

    shift, scale, gate = _split_ada(_ada(c8, l1_mix_ada_w, l1_mix_ada_b), b, d)
    u = _norm_mod(h, l1_mix_norm, scale, shift, BF16)
    tabs = _rope_tables(positions)
    n_in = l1_in_proj.shape[1]
    w_in_pad = jnp.concatenate(
        [l1_in_proj, jnp.zeros((d, q_rank + kv_rank + LANES - n_in), l1_in_proj.dtype)], axis=1).astype(BF16)
    gk_pe = _pad_lanes(l1_k_norm[QK_NOPE:], LANES)
    cq, ckv, kr, kss = _mla_in(u.reshape(t, d), w_in_pad, l1_q_a_norm, l1_kv_a_norm, gk_pe, tabs, q_rank, kv_rank)
    gq_slab = _pad_lanes(l1_q_norm, HEAD_SLAB) * (QK_HEAD ** -0.5 * LOG2_E)
    q = _q_up(cq, _q_slab_weights(l1_q_up), gq_slab, tabs, b, s)
    k, vv = _kv_up(ckv, l1_kv_up.astype(BF16), l1_k_norm[:QK_NOPE].reshape(1, QK_NOPE), kr, kss, b, s)
    o = _flash(q, k, vv)
    h = _out_res(o, l1_out_proj.astype(BF16), h, gate)

    shift, scale, gate = _split_ada(_ada(c8, l1_ffn_ada_w, l1_ffn_ada_b), b, d)
    h = _moe(h, l1_ffn_norm, scale, shift, gate, router_w, router_b,
             l1_exp_gate.astype(BF16), l1_exp_up.astype(BF16), l1_exp_down.astype(BF16))
    return h
```
